```python
import math
import jax, jax.numpy as jnp
from jax import lax
import numpy as np


D_MODEL = 4096
BATCH = 8
SEQ = 2048
DEPTH = 1
DEC_BATCH = 4
DEC_SEQ = 2048
PAST_LEN = 128

HEAD_DIM = 128
A_HEADS = 16
A_WIDTH = A_HEADS * HEAD_DIM
DILATED_PATTERNS = ((128, 1), (512, 4), (2048, 16))
B_HEADS = 8
B_WIDTH = B_HEADS * 2 * HEAD_DIM
IN_WIDTH = 3 * A_WIDTH + 3 * B_WIDTH + 2 * D_MODEL
Q_BLOCK = 128
PK_HEADS = 8
PK_DIM = 256
N_KEYS = 128
PK_TOPK = 16
N_EXPERTS = N_KEYS * N_KEYS
TOKEN_CHUNK = 128
NORM_EPS = 1e-6
NEG_INF = -1e30

kernel_name = "hybrid_dilated_diffattn_peer_encoder"


def rms_norm(x, g):
    xf = x.astype(jnp.float32)
    y = xf * lax.rsqrt(jnp.mean(xf * xf, axis=-1, keepdims=True) + NORM_EPS)
    return (y * g.astype(jnp.float32)).astype(x.dtype)


def alibi_slopes(n):
    return jnp.asarray(2.0 ** (-8.0 * np.arange(1, n + 1) / n), dtype=jnp.float32)


def dilated_window_attention(q, k, v, slopes, window, dilation):
    b, s, h, hd = q.shape
    radius = window // (2 * dilation)
    blk = radius
    unit = dilation * blk
    s_pad = -(-s // unit) * unit
    l_pad = s_pad // dilation
    nb = l_pad // blk

    def to_sub(a):
        a = jnp.pad(a, ((0, 0), (0, s_pad - s), (0, 0), (0, 0)))
        a = a.reshape(b, l_pad, dilation, h, hd).transpose(0, 2, 1, 3, 4)
        return a.reshape(b, dilation, nb, blk, h, hd)

    def band(a):
        z = jnp.zeros_like(a[:, :, :1])
        prev = jnp.concatenate([z, a[:, :, :-1]], axis=2)
        nxt = jnp.concatenate([a[:, :, 1:], z], axis=2)
        return jnp.concatenate([prev, a, nxt], axis=3)

    qs = to_sub(q)
    kb = band(to_sub(k))
    vb = band(to_sub(v))
    scores = jnp.einsum('bgnqhd,bgnkhd->bgnhqk', qs, kb).astype(jnp.float32)
    qi = jnp.arange(nb)[:, None] * blk + jnp.arange(blk)[None, :]
    ki = jnp.arange(nb)[:, None] * blk + jnp.arange(-blk, 2 * blk)[None, :]
    rel = jnp.abs(ki[:, None, :] - qi[:, :, None])
    kpos = ki[None] * dilation + jnp.arange(dilation)[:, None, None]
    key_ok = (ki >= 0)[None] & (kpos < s)
    valid = (rel <= radius)[None, :, None] & key_ok[:, :, None, None, :]
    bias = -slopes[None, :, None, None] * (dilation * rel).astype(jnp.float32)[:, None]
    scores = jnp.where(valid, scores + bias, NEG_INF)
    m = jnp.max(scores, axis=-1, keepdims=True)
    e = jnp.exp(scores - m)
    den = jnp.sum(e, axis=-1, keepdims=True)
    out = jnp.einsum('bgnhqk,bgnkhd->bgnqhd', (e / den).astype(v.dtype), vb)
    lse = (m + jnp.log(den))[..., 0]
    out = out.reshape(b, dilation, l_pad, h, hd).transpose(0, 2, 1, 3, 4).reshape(b, s_pad, h, hd)[:, :s]
    lse = lse.transpose(0, 1, 2, 4, 3).reshape(b, dilation, l_pad, h).transpose(0, 2, 1, 3).reshape(b, s_pad, h)[:, :s]
    return out, lse


def mixture_dilated_attention(q, k, v, slopes):
    outs, lses = [], []
    for window, dilation in DILATED_PATTERNS:
        o, l = dilated_window_attention(q, k, v, slopes, window, dilation)
        outs.append(o)
        lses.append(l)
    w = jax.nn.softmax(jnp.stack(lses), axis=0)
    out = jnp.sum(w[..., None] * jnp.stack(outs).astype(jnp.float32), axis=0)
    return out.astype(q.dtype)


def differential_attention(q1, q2, k1, k2, v, slopes, lam):
    b, s, h, hd = q1.shape
    nq = s // Q_BLOCK

    def blocks(a):
        return a.reshape(b, nq, Q_BLOCK, h, hd).transpose(1, 0, 2, 3, 4)

    kpos = jnp.arange(s)

    def step(args):
        i, q1b, q2b = args
        t = i * Q_BLOCK + jnp.arange(Q_BLOCK)
        dist = jnp.abs(t[:, None] - kpos[None, :]).astype(jnp.float32)
        bias = -slopes[:, None, None] * dist
        p1 = jax.nn.softmax(jnp.einsum('bqhd,bkhd->bhqk', q1b, k1).astype(jnp.float32) + bias, axis=-1)
        p2 = jax.nn.softmax(jnp.einsum('bqhd,bkhd->bhqk', q2b, k2).astype(jnp.float32) + bias, axis=-1)
        a = (p1 - lam * p2).astype(v.dtype)
        return jnp.einsum('bhqk,bkhe->bqhe', a, v)

    out = lax.map(step, (jnp.arange(nq), blocks(q1), blocks(q2)))
    return out.transpose(1, 0, 2, 3, 4).reshape(b, s, h, v.shape[-1])


def peer_ffn(x, w_query, sub_keys_1, sub_keys_2, expert_u, expert_v):
    b, s, d = x.shape
    t = b * s
    xf = x.reshape(t, d)
    q = (xf @ w_query).reshape(t, PK_HEADS, 2, PK_DIM // 2)
    s1 = jnp.einsum('thc,hnc->thn', q[:, :, 0], sub_keys_1).astype(jnp.float32)
    s2 = jnp.einsum('thc,hnc->thn', q[:, :, 1], sub_keys_2).astype(jnp.float32)
    v1, i1 = lax.top_k(s1, PK_TOPK)
    v2, i2 = lax.top_k(s2, PK_TOPK)
    cand_s = (v1[..., :, None] + v2[..., None, :]).reshape(t, PK_HEADS, PK_TOPK * PK_TOPK)
    cand_i = (i1[..., :, None] * N_KEYS + i2[..., None, :]).reshape(t, PK_HEADS, PK_TOPK * PK_TOPK)
    top_s, pos = lax.top_k(cand_s, PK_TOPK)
    idx = jnp.take_along_axis(cand_i, pos, axis=-1)
    gate = jax.nn.softmax(top_s, axis=-1)
    nc = t // TOKEN_CHUNK
    e_tok = PK_HEADS * PK_TOPK

    def step(args):
        xc, ic, gc = args
        u = expert_u[ic]
        hdn = jnp.einsum('cd,ced->ce', xc, u)
        a = (jax.nn.gelu(hdn.astype(jnp.float32)) * gc).astype(expert_v.dtype)
        return jnp.einsum('ce,ced->cd', a, expert_v[ic])

    out = lax.map(step, (xf.reshape(nc, TOKEN_CHUNK, d),
                         idx.reshape(nc, TOKEN_CHUNK, e_tok),
                         gate.reshape(nc, TOKEN_CHUNK, e_tok)))
    return out.reshape(b, s, d).astype(x.dtype)


def encoder_trunk(x, w_in, w_proj_a, w_proj_b, w_out, g_mix_norm, lam_q1, lam_k1, lam_q2, lam_k2,
                  g_subln, g_ffn_norm, w_query, sub_keys_1, sub_keys_2, expert_u, expert_v, g_final):
    b, s, _ = x.shape
    slopes_a = alibi_slopes(A_HEADS)
    slopes_b = alibi_slopes(B_HEADS)
    scale = HEAD_DIM ** -0.5
    splits = [int(c) for c in np.cumsum([A_WIDTH] * 3 + [B_WIDTH] * 3 + [D_MODEL])]
    for l in range(DEPTH):
        lam_init = 0.8 - 0.6 * math.exp(-0.3 * l)
        xn = rms_norm(x, g_mix_norm[l])
        z = jnp.einsum('bsd,de->bse', xn, w_in[l])
        qa, ka, va, qb, kb, vb, gate_a, gate_b = jnp.split(z, splits, axis=-1)
        shp_a = (b, s, A_HEADS, HEAD_DIM)
        o_a = mixture_dilated_attention(qa.reshape(shp_a) * scale, ka.reshape(shp_a), va.reshape(shp_a),
                                        slopes_a).reshape(b, s, A_WIDTH)
        qb = qb.reshape(b, s, B_HEADS, 2, HEAD_DIM) * scale
        kb = kb.reshape(b, s, B_HEADS, 2, HEAD_DIM)
        lam = (jnp.exp(jnp.sum(lam_q1[l].astype(jnp.float32) * lam_k1[l].astype(jnp.float32)))
               - jnp.exp(jnp.sum(lam_q2[l].astype(jnp.float32) * lam_k2[l].astype(jnp.float32)))
               + lam_init)
        o_b = differential_attention(qb[:, :, :, 0], qb[:, :, :, 1], kb[:, :, :, 0], kb[:, :, :, 1],
                                     vb.reshape(b, s, B_HEADS, 2 * HEAD_DIM), slopes_b, lam)
        o_b = (rms_norm(o_b, g_subln[l]) * (1.0 - lam_init)).reshape(b, s, B_WIDTH)
        merged = (jax.nn.sigmoid(gate_a) * (o_a @ w_proj_a[l])
                  + jax.nn.sigmoid(gate_b) * (o_b @ w_proj_b[l]))
        x = x + merged @ w_out[l]
        x = x + peer_ffn(rms_norm(x, g_ffn_norm[l]), w_query[l], sub_keys_1[l], sub_keys_2[l],
                         expert_u[l], expert_v[l])
    return rms_norm(x, g_final)


def setup_inputs(seed: int = 0) -> dict:
    key = jax.random.key(seed)
    ks = jax.random.split(key, 20)
    nrm = jax.random.normal
    f32 = jnp.float32
    return {
        'x_prompt': nrm(ks[0], (BATCH, SEQ, D_MODEL), f32),
        'x_sample': nrm(ks[1], (DEC_BATCH, DEC_SEQ, D_MODEL), f32),
        'w_in': nrm(ks[2], (DEPTH, D_MODEL, IN_WIDTH), f32) * D_MODEL ** -0.5,
        'w_proj_a': nrm(ks[3], (DEPTH, A_WIDTH, D_MODEL), f32) * A_WIDTH ** -0.5,
        'w_proj_b': nrm(ks[4], (DEPTH, B_WIDTH, D_MODEL), f32) * B_WIDTH ** -0.5,
        'w_out': nrm(ks[5], (DEPTH, D_MODEL, D_MODEL), f32) * D_MODEL ** -0.5,
        'g_mix_norm': 1.0 + 0.01 * nrm(ks[6], (DEPTH, D_MODEL), f32),
        'lam_q1': 0.1 * nrm(ks[7], (DEPTH, HEAD_DIM), f32),
        'lam_k1': 0.1 * nrm(ks[8], (DEPTH, HEAD_DIM), f32),
        'lam_q2': 0.1 * nrm(ks[9], (DEPTH, HEAD_DIM), f32),
        'lam_k2': 0.1 * nrm(ks[10], (DEPTH, HEAD_DIM), f32),
        'g_subln': 1.0 + 0.01 * nrm(ks[11], (DEPTH, 2 * HEAD_DIM), f32),
        'g_ffn_norm': 1.0 + 0.01 * nrm(ks[12], (DEPTH, D_MODEL), f32),
        'w_query': nrm(ks[13], (DEPTH, D_MODEL, PK_HEADS * PK_DIM), f32) * D_MODEL ** -0.5,
        'sub_keys_1': nrm(ks[14], (DEPTH, PK_HEADS, N_KEYS, PK_DIM // 2), f32) * (PK_DIM // 2) ** -0.5,
        'sub_keys_2': nrm(ks[15], (DEPTH, PK_HEADS, N_KEYS, PK_DIM // 2), f32) * (PK_DIM // 2) ** -0.5,
        'expert_u': nrm(ks[16], (DEPTH, N_EXPERTS, D_MODEL), f32) * D_MODEL ** -0.5,
        'expert_v': nrm(ks[17], (DEPTH, N_EXPERTS, D_MODEL), f32) * PK_HEADS ** -0.5,
        'g_final': 1.0 + 0.01 * nrm(ks[18], (D_MODEL,), f32),
    }


def reference(x_prompt, x_sample, w_in, w_proj_a, w_proj_b, w_out, g_mix_norm, lam_q1, lam_k1, lam_q2,
              lam_k2, g_subln, g_ffn_norm, w_query, sub_keys_1, sub_keys_2, expert_u, expert_v, g_final):
    y_prompt = encoder_trunk(x_prompt, w_in, w_proj_a, w_proj_b, w_out, g_mix_norm, lam_q1, lam_k1,
                             lam_q2, lam_k2, g_subln, g_ffn_norm, w_query, sub_keys_1, sub_keys_2,
                             expert_u, expert_v, g_final)
    y_sample = encoder_trunk(x_sample, w_in, w_proj_a, w_proj_b, w_out, g_mix_norm, lam_q1, lam_k1,
                             lam_q2, lam_k2, g_subln, g_ffn_norm, w_query, sub_keys_1, sub_keys_2,
                             expert_u, expert_v, g_final)
    return (y_prompt, y_sample)
```

```python
import functools
import math

import numpy as np
import jax
import jax.numpy as jnp
from jax import lax
from jax.experimental import pallas as pl
from jax.experimental.pallas import tpu as pltpu

HEAD_DIM = 128
DILATED_PATTERNS = ((128, 1), (512, 4), (2048, 16))
PK_TOPK = 16
NORM_EPS = 1e-6
NEG_INF = -1e30
MIB = 1024 * 1024
VMEM_LIMIT_BYTES = 56 * MIB

F32 = jnp.float32
BF16 = jnp.bfloat16
_NT = (((1,), (1,)), ((), ()))


def _params(*sem):
    return pltpu.CompilerParams(dimension_semantics=sem, vmem_limit_bytes=VMEM_LIMIT_BYTES)


def _rmsnorm_kernel(x_ref, g_ref, o_ref):
    x = x_ref[...]
    ms = jnp.mean(x * x, axis=-1, keepdims=True)
    o_ref[...] = (x * lax.rsqrt(ms + NORM_EPS) * g_ref[...]).astype(o_ref.dtype)


def _add_rmsnorm_kernel(a_ref, b_ref, g_ref, o_ref):
    x = a_ref[...] + b_ref[...]
    ms = jnp.mean(x * x, axis=-1, keepdims=True)
    o_ref[...] = (x * lax.rsqrt(ms + NORM_EPS) * g_ref[...]).astype(o_ref.dtype)


def _rmsnorm(x, g, out_dtype, residual=None, tm=256):
    t, d = x.shape
    tm = min(tm, t)
    row = pl.BlockSpec((tm, d), lambda i: (i, 0))
    gspec = pl.BlockSpec((1, d), lambda i: (0, 0))
    if residual is None:
        kern, ins, specs = _rmsnorm_kernel, (x, g.reshape(1, d)), [row, gspec]
    else:
        kern, ins, specs = _add_rmsnorm_kernel, (x, residual, g.reshape(1, d)), [row, row, gspec]
    return pl.pallas_call(
        kern,
        grid=(t // tm,),
        in_specs=specs,
        out_specs=row,
        out_shape=jax.ShapeDtypeStruct((t, d), out_dtype),
        compiler_params=_params("parallel"),
        name="rmsnorm" if residual is None else "add_rmsnorm",
    )(*ins)


def _mm_kernel(a_ref, b_ref, o_ref):
    o_ref[...] = jnp.dot(a_ref[...], b_ref[...], preferred_element_type=F32).astype(o_ref.dtype)


def _mm_colscale_kernel(a_ref, b_ref, cs_ref, o_ref):
    acc = jnp.dot(a_ref[...], b_ref[...], preferred_element_type=F32)
    o_ref[...] = (acc * cs_ref[...]).astype(o_ref.dtype)


def _mm_residual_kernel(a_ref, b_ref, r_ref, o_ref):
    acc = jnp.dot(a_ref[...], b_ref[...], preferred_element_type=F32)
    o_ref[...] = (r_ref[...] + acc).astype(o_ref.dtype)


def _matmul(a, b, out_dtype, *, n_cols=None, colscale=None, residual=None, tm=512, tn=1024, name="matmul"):
    m, k = a.shape
    n = b.shape[1] if n_cols is None else n_cols
    tm, tn = min(tm, m), min(tn, n)
    a_spec = pl.BlockSpec((tm, k), lambda j, i: (i, 0))
    b_spec = pl.BlockSpec((k, tn), lambda j, i: (0, j))
    o_spec = pl.BlockSpec((tm, tn), lambda j, i: (i, j))
    if colscale is not None:
        kern, ins = _mm_colscale_kernel, (a, b, colscale.reshape(1, n))
        specs = [a_spec, b_spec, pl.BlockSpec((1, tn), lambda j, i: (0, j))]
    elif residual is not None:
        kern, ins, specs = _mm_residual_kernel, (a, b, residual), [a_spec, b_spec, o_spec]
    else:
        kern, ins, specs = _mm_kernel, (a, b), [a_spec, b_spec]
    return pl.pallas_call(
        kern,
        grid=(n // tn, m // tm),
        in_specs=specs,
        out_specs=o_spec,
        out_shape=jax.ShapeDtypeStruct((m, n), out_dtype),
        compiler_params=_params("parallel", "parallel"),
        name=name,
    )(*ins)


def _position_tables(s):
    i = lax.broadcasted_iota(jnp.int32, (s, s), 0)
    j = lax.broadcasted_iota(jnp.int32, (s, s), 1)
    d = jnp.abs(i - j)
    count = jnp.zeros((s, s), F32)
    for window, dilation in DILATED_PATTERNS:
        count = count + ((d % dilation == 0) & (d <= window // 2)).astype(F32)
    lmask = jnp.where(count > 0, jnp.log(jnp.maximum(count, 1.0)), NEG_INF)
    return d.astype(F32), lmask


def _alibi_slopes(n):
    return jnp.asarray(2.0 ** (-8.0 * np.arange(1, n + 1) / n), dtype=F32)


def _attn_a_kernel(slopes_ref, q_ref, k_ref, v_ref, dist_ref, lmask_ref, o_ref):
    slope = slopes_ref[pl.program_id(2)]
    s = lax.dot_general(q_ref[...], k_ref[...], _NT, preferred_element_type=F32)
    s = s + (lmask_ref[...] - slope * dist_ref[...])
    m = jnp.max(s, axis=-1, keepdims=True)
    p = jnp.exp(s - m)
    l = jnp.sum(p, axis=-1, keepdims=True)
    o = jnp.dot(p.astype(BF16), v_ref[...], preferred_element_type=F32)
    o_ref[...] = (o / l).astype(o_ref.dtype)


def _attention_a(zqkv, dist, lmask, n_seq, s, n_heads, tq=512):
    tq = min(tq, s)
    nq = s // tq
    hd = HEAD_DIM
    return pl.pallas_call(
        _attn_a_kernel,
        grid=(nq, n_seq, n_heads),
        in_specs=[
            pl.BlockSpec(memory_space=pltpu.SMEM),
            pl.BlockSpec((tq, hd), lambda qi, b, h: (b * nq + qi, h)),
            pl.BlockSpec((s, hd), lambda qi, b, h: (b, n_heads + h)),
            pl.BlockSpec((s, hd), lambda qi, b, h: (b, 2 * n_heads + h)),
            pl.BlockSpec((tq, s), lambda qi, b, h: (qi, 0)),
            pl.BlockSpec((tq, s), lambda qi, b, h: (qi, 0)),
        ],
        out_specs=pl.BlockSpec((tq, hd), lambda qi, b, h: (b * nq + qi, h)),
        out_shape=jax.ShapeDtypeStruct((n_seq * s, n_heads * hd), BF16),
        compiler_params=_params("parallel", "parallel", "parallel"),
        name="attn_a",
    )(_alibi_slopes(n_heads), zqkv, zqkv, zqkv, dist, lmask)


def _attn_b_kernel(lam_init, slopes_ref, q_ref, k_ref, v_ref, dist_ref,
                   lq1_ref, lk1_ref, lq2_ref, lk2_ref, g_ref, o_ref):
    hd = HEAD_DIM
    bias = -slopes_ref[pl.program_id(2)] * dist_ref[...]
    lam = (jnp.exp(jnp.sum(lq1_ref[...] * lk1_ref[...], axis=-1, keepdims=True))
           - jnp.exp(jnp.sum(lq2_ref[...] * lk2_ref[...], axis=-1, keepdims=True)) + lam_init)

    def softmax_parts(q, k):
        s = lax.dot_general(q, k, _NT, preferred_element_type=F32) + bias
        p = jnp.exp(s - jnp.max(s, axis=-1, keepdims=True))
        return p, jnp.sum(p, axis=-1, keepdims=True)

    p1, l1 = softmax_parts(q_ref[:, :hd], k_ref[:, :hd])
    p2, l2 = softmax_parts(q_ref[:, hd:], k_ref[:, hd:])
    a = p1 * (1.0 / l1) - p2 * (lam / l2)
    o = jnp.dot(a.astype(BF16), v_ref[...], preferred_element_type=F32)
    ms = jnp.mean(o * o, axis=-1, keepdims=True)
    y = (o * lax.rsqrt(ms + NORM_EPS) * g_ref[...]) * (1.0 - lam_init)
    o_ref[...] = y.astype(o_ref.dtype)


def _attention_b(zqkv, dist, lam_vecs, g_subln, lam_init, n_seq, s, n_heads, col0, tq=512):
    tq = min(tq, s)
    nq = s // tq
    w = 2 * HEAD_DIM
    c0 = col0 // w
    vec = pl.BlockSpec((1, HEAD_DIM), lambda qi, b, h: (0, 0))
    return pl.pallas_call(
        functools.partial(_attn_b_kernel, lam_init),
        grid=(nq, n_seq, n_heads),
        in_specs=[
            pl.BlockSpec(memory_space=pltpu.SMEM),
            pl.BlockSpec((tq, w), lambda qi, b, h: (b * nq + qi, c0 + h)),
            pl.BlockSpec((s, w), lambda qi, b, h: (b, c0 + n_heads + h)),
            pl.BlockSpec((s, w), lambda qi, b, h: (b, c0 + 2 * n_heads + h)),
            pl.BlockSpec((tq, s), lambda qi, b, h: (qi, 0)),
            vec, vec, vec, vec,
            pl.BlockSpec((1, w), lambda qi, b, h: (0, 0)),
        ],
        out_specs=pl.BlockSpec((tq, w), lambda qi, b, h: (b * nq + qi, h)),
        out_shape=jax.ShapeDtypeStruct((n_seq * s, n_heads * w), BF16),
        compiler_params=_params("parallel", "parallel", "parallel"),
        name="attn_b",
    )(_alibi_slopes(n_heads), zqkv, zqkv, zqkv, dist,
      *[v.reshape(1, HEAD_DIM) for v in lam_vecs], g_subln.reshape(1, w))


def _merge_kernel(xn_ref, oa_ref, ob_ref, wga_ref, wgb_ref, wpa_ref, wpb_ref, o_ref):
    xn = xn_ref[...]
    ga = jnp.dot(xn, wga_ref[...], preferred_element_type=F32)
    gb = jnp.dot(xn, wgb_ref[...], preferred_element_type=F32)
    pa = jnp.dot(oa_ref[...], wpa_ref[...], preferred_element_type=F32)
    pb = jnp.dot(ob_ref[...], wpb_ref[...], preferred_element_type=F32)
    o_ref[...] = (jax.nn.sigmoid(ga) * pa + jax.nn.sigmoid(gb) * pb).astype(o_ref.dtype)


def _merge(xn, oa, ob, w_in, w_pa, w_pb, gate_col0, tm=512, tn=512):
    t, d = xn.shape
    tm, tn = min(tm, t), min(tn, d)
    ga0 = gate_col0 // tn
    gb0 = (gate_col0 + d) // tn
    return pl.pallas_call(
        _merge_kernel,
        grid=(t // tm, d // tn),
        in_specs=[
            pl.BlockSpec((tm, d), lambda i, j: (i, 0)),
            pl.BlockSpec((tm, oa.shape[1]), lambda i, j: (i, 0)),
            pl.BlockSpec((tm, ob.shape[1]), lambda i, j: (i, 0)),
            pl.BlockSpec((d, tn), lambda i, j: (0, ga0 + j)),
            pl.BlockSpec((d, tn), lambda i, j: (0, gb0 + j)),
            pl.BlockSpec((w_pa.shape[0], tn), lambda i, j: (0, j)),
            pl.BlockSpec((w_pb.shape[0], tn), lambda i, j: (0, j)),
        ],
        out_specs=pl.BlockSpec((tm, tn), lambda i, j: (i, j)),
        out_shape=jax.ShapeDtypeStruct((t, d), BF16),
        compiler_params=_params("parallel", "parallel"),
        name="merge",
    )(xn, oa, ob, w_in, w_in, w_pa, w_pb)


def _top_values(cur, n, row_iota):
    rows = cur.shape[0]
    out = []
    for _ in range(n):
        m = jnp.max(cur, axis=0, keepdims=True)
        out.append(m)
        first = jnp.min(jnp.where(cur == m, row_iota, rows), axis=0, keepdims=True)
        cur = jnp.where(row_iota == first, -jnp.inf, cur)
    return out


def _peer_score_kernel(q_ref, k1_ref, k2_ref, thr_ref, c_ref, s2_ref, e2_ref, v1_scr, v2_scr):
    n_heads, nk, c = k1_ref.shape
    tm = q_ref.shape[0]
    k = PK_TOPK
    iota_k = lax.broadcasted_iota(jnp.int32, (nk, tm), 0)
    pad = jnp.full((8 - 1, tm), -jnp.inf, F32)
    for h in range(n_heads):
        s1 = lax.dot_general(k1_ref[h], q_ref[:, 2 * h * c:(2 * h + 1) * c], _NT, preferred_element_type=F32)
        s2 = lax.dot_general(k2_ref[h], q_ref[:, (2 * h + 1) * c:(2 * h + 2) * c], _NT, preferred_element_type=F32)
        v1_scr[...] = jnp.concatenate(_top_values(s1, k + 1, iota_k) + [pad], axis=0)
        v2_scr[...] = jnp.concatenate(_top_values(s2, k + 1, iota_k) + [pad], axis=0)
        pieces = [v1_scr[0:1, :] + v2_scr[0:k, :]]
        for a in range(1, 8):
            pieces.append(v1_scr[a:a + 1, :] + v2_scr[0:8, :])
        pieces.append(v1_scr[8:k, :] + v2_scr[0:1, :])
        pieces.append(v1_scr[k:k + 8, :] + v2_scr[0:1, :])
        pieces.append(v1_scr[0:1, :] + v2_scr[k:k + 8, :])
        cand = jnp.concatenate(pieces, axis=0)
        iota_c = lax.broadcasted_iota(jnp.int32, cand.shape, 0)
        tops = _top_values(cand, k + 1, iota_c)
        tau = 0.5 * (tops[k - 1] + tops[k])
        z = jnp.exp(tops[0] - tops[0])
        for t in tops[1:k]:
            z = z + jnp.exp(t - tops[0])
        thr_ref[h] = tau - s1
        c_ref[h] = jnp.exp(s1 - v1_scr[0:1, :]) / z
        s2_ref[h] = s2
        e2_ref[h] = jnp.exp(s2 - v2_scr[0:1, :])


def _peer_scores(q, k1, k2, tm=512):
    t = q.shape[0]
    n_heads, nk, c = k1.shape
    tm = min(tm, t)
    out = jax.ShapeDtypeStruct((n_heads, nk, t), F32)
    ospec = pl.BlockSpec((n_heads, nk, tm), lambda i: (0, 0, i))
    kspec = pl.BlockSpec((n_heads, nk, c), lambda i: (0, 0, 0))
    return pl.pallas_call(
        _peer_score_kernel,
        grid=(t // tm,),
        in_specs=[pl.BlockSpec((tm, q.shape[1]), lambda i: (i, 0)), kspec, kspec],
        out_specs=[ospec] * 4,
        out_shape=[out] * 4,
        scratch_shapes=[pltpu.VMEM((PK_TOPK + 8, tm), F32), pltpu.VMEM((PK_TOPK + 8, tm), F32)],
        compiler_params=_params("parallel"),
        name="peer_scores",
    )(q, k1, k2)


def _peer_expert_kernel(xn_ref, u_ref, v_ref, thr_ref, c_ref, s2_ref, e2_ref, o_ref):
    j = pl.program_id(1)
    n_heads, nk, tm = s2_ref.shape
    te = u_ref.shape[0]
    hidden = lax.dot_general(xn_ref[...], u_ref[...], _NT, preferred_element_type=F32)
    act = jax.nn.gelu(hidden)
    gates = []
    for r in range(te // nk):
        i1 = j * (te // nk) + r
        g_t = jnp.zeros((nk, tm), F32)
        for h in range(n_heads):
            thr = thr_ref[h, pl.ds(i1, 1), :]
            coef = c_ref[h, pl.ds(i1, 1), :]
            g_t = g_t + jnp.where(s2_ref[h] >= thr, e2_ref[h] * coef, 0.0)
        gates.append(g_t.T)
    a = (act * jnp.concatenate(gates, axis=1)).astype(BF16)
    contrib = jnp.dot(a, v_ref[...], preferred_element_type=F32)

    @pl.when(j == 0)
    def _():
        o_ref[...] = contrib

    @pl.when(j > 0)
    def _():
        o_ref[...] += contrib


def _peer_experts(xn, u, v, thr, coef, s2, e2, tm=512, te=512):
    t, d = xn.shape
    n_exp = u.shape[0]
    n_heads, nk, _ = s2.shape
    tm, te = min(tm, t), min(te, n_exp)
    once = pl.Buffered(1)
    gspec = pl.BlockSpec((n_heads, nk, tm), lambda i, j: (0, 0, i), pipeline_mode=once)
    return pl.pallas_call(
        _peer_expert_kernel,
        grid=(t // tm, n_exp // te),
        in_specs=[
            pl.BlockSpec((tm, d), lambda i, j: (i, 0), pipeline_mode=once),
            pl.BlockSpec((te, d), lambda i, j: (j, 0)),
            pl.BlockSpec((te, d), lambda i, j: (j, 0)),
            gspec, gspec, gspec, gspec,
        ],
        out_specs=pl.BlockSpec((tm, d), lambda i, j: (i, 0)),
        out_shape=jax.ShapeDtypeStruct((t, d), F32),
        compiler_params=_params("parallel", "arbitrary"),
        name="peer_experts",
    )(xn, u, v, thr, coef, s2, e2)


def kernel(x_prompt, x_sample, w_in, w_proj_a, w_proj_b, w_out, g_mix_norm, lam_q1, lam_k1, lam_q2,
           lam_k2, g_subln, g_ffn_norm, w_query, sub_keys_1, sub_keys_2, expert_u, expert_v, g_final):
    s, d = x_prompt.shape[1], x_prompt.shape[2]
    assert x_sample.shape[1] == s and all(s % (w // 2) == 0 for w, _ in DILATED_PATTERNS)
    n_seq = x_prompt.shape[0] + x_sample.shape[0]
    depth = w_in.shape[0]
    a_width, b_width = w_proj_a.shape[1], w_proj_b.shape[1]
    a_heads, b_heads = a_width // HEAD_DIM, b_width // (2 * HEAD_DIM)
    qkv_width = 3 * a_width + 3 * b_width
    scale = HEAD_DIM ** -0.5

    x = jnp.concatenate([x_prompt, x_sample], axis=0).reshape(n_seq * s, d)
    dist, lmask = _position_tables(s)
    one_a, one_b = jnp.ones((2 * a_width,), F32), jnp.ones((2 * b_width,), F32)
    colscale = jnp.concatenate([jnp.full((a_width,), scale, F32), one_a, jnp.full((b_width,), scale, F32), one_b])

    for l in range(depth):
        lam_init = 0.8 - 0.6 * math.exp(-0.3 * l)
        w_in_l = w_in[l].astype(BF16)
        xn = _rmsnorm(x, g_mix_norm[l], BF16)
        zqkv = _matmul(xn, w_in_l, BF16, n_cols=qkv_width, colscale=colscale, name="in_proj")
        o_a = _attention_a(zqkv, dist, lmask, n_seq, s, a_heads)
        o_b = _attention_b(zqkv, dist, (lam_q1[l], lam_k1[l], lam_q2[l], lam_k2[l]), g_subln[l], lam_init,
                           n_seq, s, b_heads, 3 * a_width)
        merged = _merge(xn, o_a, o_b, w_in_l, w_proj_a[l].astype(BF16), w_proj_b[l].astype(BF16), qkv_width)
        x = _matmul(merged, w_out[l].astype(BF16), F32, residual=x, name="out_proj")
        xn2 = _rmsnorm(x, g_ffn_norm[l], BF16)
        q = _matmul(xn2, w_query[l].astype(BF16), BF16, name="peer_query")
        thr, coef, s2, e2 = _peer_scores(q, sub_keys_1[l].astype(BF16), sub_keys_2[l].astype(BF16))
        peer = _peer_experts(xn2, expert_u[l].astype(BF16), expert_v[l].astype(BF16), thr, coef, s2, e2)
        if l + 1 < depth:
            x = x + peer
    y = _rmsnorm(x, g_final, F32, residual=peer).reshape(n_seq, s, d)
    return y[:x_prompt.shape[0]], y[x_prompt.shape[0]:]
```

```python
import functools
import math

import numpy as np
import jax
import jax.numpy as jnp
from jax import lax
from jax.experimental import pallas as pl
from jax.experimental.pallas import tpu as pltpu

HEAD_DIM = 128
DILATED_PATTERNS = ((128, 1), (512, 4), (2048, 16))
PK_TOPK = 16
NORM_EPS = 1e-6
NEG_INF = -1e30
MIB = 1024 * 1024
VMEM_LIMIT_BYTES = 56 * MIB

F32 = jnp.float32
BF16 = jnp.bfloat16
_NT = (((1,), (1,)), ((), ()))


def _params(*sem):
    return pltpu.CompilerParams(dimension_semantics=sem, vmem_limit_bytes=VMEM_LIMIT_BYTES)


def _chunk_specs(chunks, tm, tn, row_col):
    specs, bounds, lo = [], [], 0
    for rows in chunks:
        n = rows // tm

        def index_map(*g, lo=lo, n=n):
            i, j = row_col(*g)
            return (jnp.clip(i - lo, 0, n - 1), j)

        specs.append(pl.BlockSpec((tm, tn), index_map))
        bounds.append((lo, lo + n))
        lo += n
    return specs, bounds


def _for_active_chunk(i, refs, bounds, body):
    if len(refs) == 1:
        body(refs[0])
        return
    for ref, (lo, hi) in zip(refs, bounds):
        pl.when((i >= lo) & (i < hi))(functools.partial(body, ref))


def _rms(x, g):
    ms = jnp.mean(x * x, axis=-1, keepdims=True)
    return x * lax.rsqrt(ms + NORM_EPS) * g


def _rmsnorm_kernel(bounds, *refs):
    *x_refs, g_ref, o_ref = refs

    def body(x_ref):
        o_ref[...] = _rms(x_ref[...], g_ref[...]).astype(o_ref.dtype)

    _for_active_chunk(pl.program_id(0), x_refs, bounds, body)


def _rmsnorm(chunks, g, out_dtype, tm=256):
    d = chunks[0].shape[1]
    t = sum(c.shape[0] for c in chunks)
    specs, bounds = _chunk_specs([c.shape[0] for c in chunks], tm, d, lambda i: (i, 0))
    return pl.pallas_call(
        functools.partial(_rmsnorm_kernel, bounds),
        grid=(t // tm,),
        in_specs=specs + [pl.BlockSpec((1, d), lambda i: (0, 0))],
        out_specs=pl.BlockSpec((tm, d), lambda i: (i, 0)),
        out_shape=jax.ShapeDtypeStruct((t, d), out_dtype),
        compiler_params=_params("parallel"),
        name="rmsnorm",
    )(*chunks, g.reshape(1, d))


def _add_rmsnorm_kernel(bounds, a_ref, b_ref, g_ref, *o_refs):
    y = _rms(a_ref[...] + b_ref[...], g_ref[...])

    def body(o_ref):
        o_ref[...] = y.astype(o_ref.dtype)

    _for_active_chunk(pl.program_id(0), o_refs, bounds, body)


def _add_rmsnorm_split(a, b, g, out_rows, out_dtype, tm=256):
    t, d = a.shape
    row = pl.BlockSpec((tm, d), lambda i: (i, 0))
    specs, bounds = _chunk_specs(out_rows, tm, d, lambda i: (i, 0))
    return pl.pallas_call(
        functools.partial(_add_rmsnorm_kernel, bounds),
        grid=(t // tm,),
        in_specs=[row, row, pl.BlockSpec((1, d), lambda i: (0, 0))],
        out_specs=specs,
        out_shape=[jax.ShapeDtypeStruct((r, d), out_dtype) for r in out_rows],
        compiler_params=_params("arbitrary"),
        name="add_rmsnorm",
    )(a, b, g.reshape(1, d))


def _mm_kernel(a_ref, b_ref, o_ref):
    o_ref[...] = jnp.dot(a_ref[...], b_ref[...], preferred_element_type=F32).astype(o_ref.dtype)


def _mm_colscale_kernel(a_ref, b_ref, cs_ref, o_ref):
    acc = jnp.dot(a_ref[...], b_ref[...], preferred_element_type=F32)
    o_ref[...] = (acc * cs_ref[...]).astype(o_ref.dtype)


def _mm_residual_kernel(bounds, a_ref, b_ref, *refs):
    *r_refs, o_ref = refs
    acc = jnp.dot(a_ref[...], b_ref[...], preferred_element_type=F32)

    def body(r_ref):
        o_ref[...] = (r_ref[...] + acc).astype(o_ref.dtype)

    _for_active_chunk(pl.program_id(1), r_refs, bounds, body)


def _matmul(a, b, out_dtype, *, n_cols=None, colscale=None, residual=None, tm=512, tn=1024, name="matmul"):
    m, k = a.shape
    n = b.shape[1] if n_cols is None else n_cols
    tm, tn = min(tm, m), min(tn, n)
    a_spec = pl.BlockSpec((tm, k), lambda j, i: (i, 0))
    b_spec = pl.BlockSpec((k, tn), lambda j, i: (0, j))
    o_spec = pl.BlockSpec((tm, tn), lambda j, i: (i, j))
    if colscale is not None:
        kern, ins = _mm_colscale_kernel, (a, b, colscale.reshape(1, n))
        specs = [a_spec, b_spec, pl.BlockSpec((1, tn), lambda j, i: (0, j))]
    elif residual is not None:
        r_specs, bounds = _chunk_specs([r.shape[0] for r in residual], tm, tn, lambda j, i: (i, j))
        kern, ins, specs = functools.partial(_mm_residual_kernel, bounds), (a, b, *residual), [a_spec, b_spec] + r_specs
    else:
        kern, ins, specs = _mm_kernel, (a, b), [a_spec, b_spec]
    return pl.pallas_call(
        kern,
        grid=(n // tn, m // tm),
        in_specs=specs,
        out_specs=o_spec,
        out_shape=jax.ShapeDtypeStruct((m, n), out_dtype),
        compiler_params=_params("parallel", "parallel"),
        name=name,
    )(*ins)


def _position_tables(s):
    i = lax.broadcasted_iota(jnp.int32, (s, s), 0)
    j = lax.broadcasted_iota(jnp.int32, (s, s), 1)
    d = jnp.abs(i - j)
    count = jnp.zeros((s, s), F32)
    for window, dilation in DILATED_PATTERNS:
        count = count + ((d % dilation == 0) & (d <= window // 2)).astype(F32)
    lmask = jnp.where(count > 0, jnp.log(jnp.maximum(count, 1.0)), NEG_INF)
    return d.astype(F32), lmask


def _alibi_slopes(n):
    return jnp.asarray(2.0 ** (-8.0 * np.arange(1, n + 1) / n), dtype=F32)


def _attn_a_kernel(slopes_ref, q_ref, k_ref, v_ref, dist_ref, lmask_ref, o_ref):
    slope = slopes_ref[pl.program_id(2)]
    s = lax.dot_general(q_ref[...], k_ref[...], _NT, preferred_element_type=F32)
    s = s + (lmask_ref[...] - slope * dist_ref[...])
    m = jnp.max(s, axis=-1, keepdims=True)
    p = jnp.exp(s - m)
    l = jnp.sum(p, axis=-1, keepdims=True)
    o = jnp.dot(p.astype(BF16), v_ref[...], preferred_element_type=F32)
    o_ref[...] = (o / l).astype(o_ref.dtype)


def _attention_a(zqkv, dist, lmask, n_seq, s, n_heads, tq=512):
    tq = min(tq, s)
    nq = s // tq
    hd = HEAD_DIM
    return pl.pallas_call(
        _attn_a_kernel,
        grid=(nq, n_seq, n_heads),
        in_specs=[
            pl.BlockSpec(memory_space=pltpu.SMEM),
            pl.BlockSpec((tq, hd), lambda qi, b, h: (b * nq + qi, h)),
            pl.BlockSpec((s, hd), lambda qi, b, h: (b, n_heads + h)),
            pl.BlockSpec((s, hd), lambda qi, b, h: (b, 2 * n_heads + h)),
            pl.BlockSpec((tq, s), lambda qi, b, h: (qi, 0)),
            pl.BlockSpec((tq, s), lambda qi, b, h: (qi, 0)),
        ],
        out_specs=pl.BlockSpec((tq, hd), lambda qi, b, h: (b * nq + qi, h)),
        out_shape=jax.ShapeDtypeStruct((n_seq * s, n_heads * hd), BF16),
        compiler_params=_params("parallel", "parallel", "parallel"),
        name="attn_a",
    )(_alibi_slopes(n_heads), zqkv, zqkv, zqkv, dist, lmask)


def _attn_b_kernel(lam_init, slopes_ref, q_ref, k_ref, v_ref, dist_ref,
                   lq1_ref, lk1_ref, lq2_ref, lk2_ref, g_ref, o_ref):
    hd = HEAD_DIM
    bias = -slopes_ref[pl.program_id(2)] * dist_ref[...]
    lam = (jnp.exp(jnp.sum(lq1_ref[...] * lk1_ref[...], axis=-1, keepdims=True))
           - jnp.exp(jnp.sum(lq2_ref[...] * lk2_ref[...], axis=-1, keepdims=True)) + lam_init)

    def softmax_parts(q, k):
        s = lax.dot_general(q, k, _NT, preferred_element_type=F32) + bias
        p = jnp.exp(s - jnp.max(s, axis=-1, keepdims=True))
        return p, jnp.sum(p, axis=-1, keepdims=True)

    p1, l1 = softmax_parts(q_ref[:, :hd], k_ref[:, :hd])
    p2, l2 = softmax_parts(q_ref[:, hd:], k_ref[:, hd:])
    a = p1 * (1.0 / l1) - p2 * (lam / l2)
    o = jnp.dot(a.astype(BF16), v_ref[...], preferred_element_type=F32)
    ms = jnp.mean(o * o, axis=-1, keepdims=True)
    y = (o * lax.rsqrt(ms + NORM_EPS) * g_ref[...]) * (1.0 - lam_init)
    o_ref[...] = y.astype(o_ref.dtype)


def _attention_b(zqkv, dist, lam_vecs, g_subln, lam_init, n_seq, s, n_heads, col0, tq=512):
    tq = min(tq, s)
    nq = s // tq
    w = 2 * HEAD_DIM
    c0 = col0 // w
    vec = pl.BlockSpec((1, HEAD_DIM), lambda qi, b, h: (0, 0))
    return pl.pallas_call(
        functools.partial(_attn_b_kernel, lam_init),
        grid=(nq, n_seq, n_heads),
        in_specs=[
            pl.BlockSpec(memory_space=pltpu.SMEM),
            pl.BlockSpec((tq, w), lambda qi, b, h: (b * nq + qi, c0 + h)),
            pl.BlockSpec((s, w), lambda qi, b, h: (b, c0 + n_heads + h)),
            pl.BlockSpec((s, w), lambda qi, b, h: (b, c0 + 2 * n_heads + h)),
            pl.BlockSpec((tq, s), lambda qi, b, h: (qi, 0)),
            vec, vec, vec, vec,
            pl.BlockSpec((1, w), lambda qi, b, h: (0, 0)),
        ],
        out_specs=pl.BlockSpec((tq, w), lambda qi, b, h: (b * nq + qi, h)),
        out_shape=jax.ShapeDtypeStruct((n_seq * s, n_heads * w), BF16),
        compiler_params=_params("parallel", "parallel", "parallel"),
        name="attn_b",
    )(_alibi_slopes(n_heads), zqkv, zqkv, zqkv, dist,
      *[v.reshape(1, HEAD_DIM) for v in lam_vecs], g_subln.reshape(1, w))


def _merge_kernel(xn_ref, oa_ref, ob_ref, wga_ref, wgb_ref, wpa_ref, wpb_ref, o_ref):
    xn = xn_ref[...]
    ga = jnp.dot(xn, wga_ref[...], preferred_element_type=F32)
    gb = jnp.dot(xn, wgb_ref[...], preferred_element_type=F32)
    pa = jnp.dot(oa_ref[...], wpa_ref[...], preferred_element_type=F32)
    pb = jnp.dot(ob_ref[...], wpb_ref[...], preferred_element_type=F32)
    o_ref[...] = (jax.nn.sigmoid(ga) * pa + jax.nn.sigmoid(gb) * pb).astype(o_ref.dtype)


def _merge(xn, oa, ob, w_in, w_pa, w_pb, gate_col0, tm=512, tn=512):
    t, d = xn.shape
    tm, tn = min(tm, t), min(tn, d)
    ga0 = gate_col0 // tn
    gb0 = (gate_col0 + d) // tn
    return pl.pallas_call(
        _merge_kernel,
        grid=(t // tm, d // tn),
        in_specs=[
            pl.BlockSpec((tm, d), lambda i, j: (i, 0)),
            pl.BlockSpec((tm, oa.shape[1]), lambda i, j: (i, 0)),
            pl.BlockSpec((tm, ob.shape[1]), lambda i, j: (i, 0)),
            pl.BlockSpec((d, tn), lambda i, j: (0, ga0 + j)),
            pl.BlockSpec((d, tn), lambda i, j: (0, gb0 + j)),
            pl.BlockSpec((w_pa.shape[0], tn), lambda i, j: (0, j)),
            pl.BlockSpec((w_pb.shape[0], tn), lambda i, j: (0, j)),
        ],
        out_specs=pl.BlockSpec((tm, tn), lambda i, j: (i, j)),
        out_shape=jax.ShapeDtypeStruct((t, d), BF16),
        compiler_params=_params("parallel", "parallel"),
        name="merge",
    )(xn, oa, ob, w_in, w_in, w_pa, w_pb)


def _top_values(cur, n, row_iota):
    rows = cur.shape[0]
    out = []
    for _ in range(n):
        m = jnp.max(cur, axis=0, keepdims=True)
        out.append(m)
        first = jnp.min(jnp.where(cur == m, row_iota, rows), axis=0, keepdims=True)
        cur = jnp.where(row_iota == first, -jnp.inf, cur)
    return out


def _peer_score_kernel(q_ref, k1_ref, k2_ref, thr_ref, c_ref, s2_ref, e2_ref, v1_scr, v2_scr):
    n_heads, nk, c = k1_ref.shape
    tm = q_ref.shape[0]
    k = PK_TOPK
    iota_k = lax.broadcasted_iota(jnp.int32, (nk, tm), 0)
    pad = jnp.full((8 - 1, tm), -jnp.inf, F32)
    for h in range(n_heads):
        s1 = lax.dot_general(k1_ref[h], q_ref[:, 2 * h * c:(2 * h + 1) * c], _NT, preferred_element_type=F32)
        s2 = lax.dot_general(k2_ref[h], q_ref[:, (2 * h + 1) * c:(2 * h + 2) * c], _NT, preferred_element_type=F32)
        v1_scr[...] = jnp.concatenate(_top_values(s1, k + 1, iota_k) + [pad], axis=0)
        v2_scr[...] = jnp.concatenate(_top_values(s2, k + 1, iota_k) + [pad], axis=0)
        pieces = [v1_scr[0:1, :] + v2_scr[0:k, :]]
        for a in range(1, 8):
            pieces.append(v1_scr[a:a + 1, :] + v2_scr[0:8, :])
        pieces.append(v1_scr[8:k, :] + v2_scr[0:1, :])
        pieces.append(v1_scr[k:k + 8, :] + v2_scr[0:1, :])
        pieces.append(v1_scr[0:1, :] + v2_scr[k:k + 8, :])
        cand = jnp.concatenate(pieces, axis=0)
        iota_c = lax.broadcasted_iota(jnp.int32, cand.shape, 0)
        tops = _top_values(cand, k + 1, iota_c)
        tau = 0.5 * (tops[k - 1] + tops[k])
        z = jnp.exp(tops[0] - tops[0])
        for t in tops[1:k]:
            z = z + jnp.exp(t - tops[0])
        thr_ref[h] = tau - s1
        c_ref[h] = jnp.exp(s1 - v1_scr[0:1, :]) / z
        s2_ref[h] = s2
        e2_ref[h] = jnp.exp(s2 - v2_scr[0:1, :])


def _peer_scores(q, k1, k2, tm=512):
    t = q.shape[0]
    n_heads, nk, c = k1.shape
    tm = min(tm, t)
    out = jax.ShapeDtypeStruct((n_heads, nk, t), F32)
    ospec = pl.BlockSpec((n_heads, nk, tm), lambda i: (0, 0, i))
    kspec = pl.BlockSpec((n_heads, nk, c), lambda i: (0, 0, 0))
    return pl.pallas_call(
        _peer_score_kernel,
        grid=(t // tm,),
        in_specs=[pl.BlockSpec((tm, q.shape[1]), lambda i: (i, 0)), kspec, kspec],
        out_specs=[ospec] * 4,
        out_shape=[out] * 4,
        scratch_shapes=[pltpu.VMEM((PK_TOPK + 8, tm), F32), pltpu.VMEM((PK_TOPK + 8, tm), F32)],
        compiler_params=_params("parallel"),
        name="peer_scores",
    )(q, k1, k2)


def _peer_expert_kernel(xn_ref, u_ref, v_ref, thr_ref, c_ref, s2_ref, e2_ref, o_ref):
    j = pl.program_id(1)
    n_heads, nk, tm = s2_ref.shape
    te = u_ref.shape[0]
    @pl.when(j == 0)
    def _():
        o_ref[...] = jnp.zeros_like(o_ref)

    gates = []
    for r in range(te // nk):
        i1 = j * (te // nk) + r
        g_t = jnp.zeros((nk, tm), F32)
        for h in range(n_heads):
            thr = thr_ref[h, pl.ds(i1, 1), :]
            coef = c_ref[h, pl.ds(i1, 1), :]
            g_t = g_t + jnp.where(s2_ref[h] >= thr, e2_ref[h] * coef, 0.0)
        gates.append(g_t.T)
    hidden = lax.dot_general(xn_ref[...], u_ref[...], _NT, preferred_element_type=F32)
    a = (jax.nn.gelu(hidden) * jnp.concatenate(gates, axis=1)).astype(BF16)
    o_ref[...] += jnp.dot(a, v_ref[...], preferred_element_type=F32)


def _peer_experts(xn, u, v, thr, coef, s2, e2, tm=512, te=512):
    t, d = xn.shape
    n_exp = u.shape[0]
    n_heads, nk, _ = s2.shape
    tm, te = min(tm, t), min(te, n_exp)
    once = pl.Buffered(1)
    gspec = pl.BlockSpec((n_heads, nk, tm), lambda i, j: (0, 0, i), pipeline_mode=once)
    return pl.pallas_call(
        _peer_expert_kernel,
        grid=(t // tm, n_exp // te),
        in_specs=[
            pl.BlockSpec((tm, d), lambda i, j: (i, 0), pipeline_mode=once),
            pl.BlockSpec((te, d), lambda i, j: (j, 0)),
            pl.BlockSpec((te, d), lambda i, j: (j, 0)),
            gspec, gspec, gspec, gspec,
        ],
        out_specs=pl.BlockSpec((tm, d), lambda i, j: (i, 0)),
        out_shape=jax.ShapeDtypeStruct((t, d), F32),
        compiler_params=_params("parallel", "arbitrary"),
        name="peer_experts",
    )(xn, u, v, thr, coef, s2, e2)


def kernel(x_prompt, x_sample, w_in, w_proj_a, w_proj_b, w_out, g_mix_norm, lam_q1, lam_k1, lam_q2,
           lam_k2, g_subln, g_ffn_norm, w_query, sub_keys_1, sub_keys_2, expert_u, expert_v, g_final):
    s, d = x_prompt.shape[1], x_prompt.shape[2]
    assert x_sample.shape[1] == s and all(s % (w // 2) == 0 for w, _ in DILATED_PATTERNS)
    n_seq = x_prompt.shape[0] + x_sample.shape[0]
    depth = w_in.shape[0]
    a_width, b_width = w_proj_a.shape[1], w_proj_b.shape[1]
    a_heads, b_heads = a_width // HEAD_DIM, b_width // (2 * HEAD_DIM)
    qkv_width = 3 * a_width + 3 * b_width
    scale = HEAD_DIM ** -0.5

    x_chunks = [x_prompt.reshape(-1, d), x_sample.reshape(-1, d)]
    out_rows = [c.shape[0] for c in x_chunks]
    dist, lmask = _position_tables(s)
    one_a, one_b = jnp.ones((2 * a_width,), F32), jnp.ones((2 * b_width,), F32)
    colscale = jnp.concatenate([jnp.full((a_width,), scale, F32), one_a, jnp.full((b_width,), scale, F32), one_b])

    for l in range(depth):
        lam_init = 0.8 - 0.6 * math.exp(-0.3 * l)
        w_in_l = w_in[l].astype(BF16)
        xn = _rmsnorm(x_chunks, g_mix_norm[l], BF16)
        zqkv = _matmul(xn, w_in_l, BF16, n_cols=qkv_width, colscale=colscale, name="in_proj")
        o_a = _attention_a(zqkv, dist, lmask, n_seq, s, a_heads)
        o_b = _attention_b(zqkv, dist, (lam_q1[l], lam_k1[l], lam_q2[l], lam_k2[l]), g_subln[l], lam_init,
                           n_seq, s, b_heads, 3 * a_width)
        merged = _merge(xn, o_a, o_b, w_in_l, w_proj_a[l].astype(BF16), w_proj_b[l].astype(BF16), qkv_width)
        x1 = _matmul(merged, w_out[l].astype(BF16), F32, residual=x_chunks, name="out_proj")
        xn2 = _rmsnorm([x1], g_ffn_norm[l], BF16)
        q = _matmul(xn2, w_query[l].astype(BF16), BF16, name="peer_query")
        thr, coef, s2, e2 = _peer_scores(q, sub_keys_1[l].astype(BF16), sub_keys_2[l].astype(BF16))
        peer = _peer_experts(xn2, expert_u[l].astype(BF16), expert_v[l].astype(BF16), thr, coef, s2, e2)
        if l + 1 < depth:
            x_chunks = [x1 + peer]
    y_prompt, y_sample = _add_rmsnorm_split(x1, peer, g_final, out_rows, F32)
    return y_prompt.reshape(x_prompt.shape), y_sample.reshape(x_sample.shape)
```

```python
import functools
import math

import numpy as np
import jax
import jax.numpy as jnp
from jax import lax
from jax.experimental import pallas as pl
from jax.experimental.pallas import tpu as pltpu

HEAD_DIM = 128
DILATED_PATTERNS = ((128, 1), (512, 4), (2048, 16))
PK_TOPK = 16
NORM_EPS = 1e-6
NEG_INF = -1e30
MIB = 1024 * 1024
VMEM_LIMIT_BYTES = 56 * MIB

F32 = jnp.float32
BF16 = jnp.bfloat16
_NT = (((1,), (1,)), ((), ()))


def _params(*sem):
    return pltpu.CompilerParams(dimension_semantics=sem, vmem_limit_bytes=VMEM_LIMIT_BYTES)


def _chunk_specs(chunks, tm, tn, row_col):
    specs, bounds, lo = [], [], 0
    for rows in chunks:
        n = rows // tm

        def index_map(*g, lo=lo, n=n):
            i, j = row_col(*g)
            return (jnp.clip(i - lo, 0, n - 1), j)

        specs.append(pl.BlockSpec((tm, tn), index_map))
        bounds.append((lo, lo + n))
        lo += n
    return specs, bounds


def _for_active_chunk(i, refs, bounds, body):
    if len(refs) == 1:
        body(refs[0])
        return
    for ref, (lo, hi) in zip(refs, bounds):
        pl.when((i >= lo) & (i < hi))(functools.partial(body, ref))


def _rms(x, g):
    ms = jnp.mean(x * x, axis=-1, keepdims=True)
    return x * lax.rsqrt(ms + NORM_EPS) * g


def _rmsnorm_kernel(bounds, *refs):
    *x_refs, g_ref, o_ref = refs

    def body(x_ref):
        o_ref[...] = _rms(x_ref[...], g_ref[...]).astype(o_ref.dtype)

    _for_active_chunk(pl.program_id(0), x_refs, bounds, body)


def _rmsnorm(chunks, g, out_dtype, tm=256):
    d = chunks[0].shape[1]
    t = sum(c.shape[0] for c in chunks)
    specs, bounds = _chunk_specs([c.shape[0] for c in chunks], tm, d, lambda i: (i, 0))
    return pl.pallas_call(
        functools.partial(_rmsnorm_kernel, bounds),
        grid=(t // tm,),
        in_specs=specs + [pl.BlockSpec((1, d), lambda i: (0, 0))],
        out_specs=pl.BlockSpec((tm, d), lambda i: (i, 0)),
        out_shape=jax.ShapeDtypeStruct((t, d), out_dtype),
        compiler_params=_params("parallel"),
        name="rmsnorm",
    )(*chunks, g.reshape(1, d))


def _add_rmsnorm_kernel(bounds, a_ref, b_ref, g_ref, *o_refs):
    y = _rms(a_ref[...] + b_ref[...], g_ref[...])

    def body(o_ref):
        o_ref[...] = y.astype(o_ref.dtype)

    _for_active_chunk(pl.program_id(0), o_refs, bounds, body)


def _add_rmsnorm_split(a, b, g, out_rows, out_dtype, tm=256):
    t, d = a.shape
    row = pl.BlockSpec((tm, d), lambda i: (i, 0))
    specs, bounds = _chunk_specs(out_rows, tm, d, lambda i: (i, 0))
    return pl.pallas_call(
        functools.partial(_add_rmsnorm_kernel, bounds),
        grid=(t // tm,),
        in_specs=[row, row, pl.BlockSpec((1, d), lambda i: (0, 0))],
        out_specs=specs,
        out_shape=[jax.ShapeDtypeStruct((r, d), out_dtype) for r in out_rows],
        compiler_params=_params("arbitrary"),
        name="add_rmsnorm",
    )(a, b, g.reshape(1, d))


def _mm_kernel(a_ref, b_ref, o_ref):
    o_ref[...] = jnp.dot(a_ref[...], b_ref[...], preferred_element_type=F32).astype(o_ref.dtype)


def _mm_colscale_kernel(a_ref, b_ref, cs_ref, o_ref):
    acc = jnp.dot(a_ref[...], b_ref[...], preferred_element_type=F32)
    o_ref[...] = (acc * cs_ref[...]).astype(o_ref.dtype)


def _mm_residual_kernel(bounds, a_ref, b_ref, *refs):
    *r_refs, o_ref = refs
    acc = jnp.dot(a_ref[...], b_ref[...], preferred_element_type=F32)

    def body(r_ref):
        o_ref[...] = (r_ref[...] + acc).astype(o_ref.dtype)

    _for_active_chunk(pl.program_id(1), r_refs, bounds, body)


def _matmul(a, b, out_dtype, *, n_cols=None, colscale=None, residual=None, tm=512, tn=1024, name="matmul"):
    m, k = a.shape
    n = b.shape[1] if n_cols is None else n_cols
    tm, tn = min(tm, m), min(tn, n)
    a_spec = pl.BlockSpec((tm, k), lambda j, i: (i, 0))
    b_spec = pl.BlockSpec((k, tn), lambda j, i: (0, j))
    o_spec = pl.BlockSpec((tm, tn), lambda j, i: (i, j))
    if colscale is not None:
        kern, ins = _mm_colscale_kernel, (a, b, colscale.reshape(1, n))
        specs = [a_spec, b_spec, pl.BlockSpec((1, tn), lambda j, i: (0, j))]
    elif residual is not None:
        r_specs, bounds = _chunk_specs([r.shape[0] for r in residual], tm, tn, lambda j, i: (i, j))
        kern, ins, specs = functools.partial(_mm_residual_kernel, bounds), (a, b, *residual), [a_spec, b_spec] + r_specs
    else:
        kern, ins, specs = _mm_kernel, (a, b), [a_spec, b_spec]
    return pl.pallas_call(
        kern,
        grid=(n // tn, m // tm),
        in_specs=specs,
        out_specs=o_spec,
        out_shape=jax.ShapeDtypeStruct((m, n), out_dtype),
        compiler_params=_params("parallel", "parallel"),
        name=name,
    )(*ins)


def _distance_table(s):
    i = lax.broadcasted_iota(jnp.int32, (s, s), 0)
    j = lax.broadcasted_iota(jnp.int32, (s, s), 1)
    return jnp.abs(i - j).astype(F32)


def _alibi_slopes(n):
    return jnp.asarray(2.0 ** (-8.0 * np.arange(1, n + 1) / n), dtype=F32)


A_STRIDE = 4
A_QBLOCK = 128
A_LOCAL_WINDOW = 2 * A_QBLOCK
A_GROUP = 4


def _attn_a_tables(s):
    local = [(w, d) for w, d in DILATED_PATTERNS if d == 1]
    strided = [(w, d) for w, d in DILATED_PATTERNS if d != 1]
    assert all(d % A_STRIDE == 0 for _, d in strided) and all(w // 2 <= A_QBLOCK // 2 for w, _ in local)

    def log_count(dist, patterns):
        count = jnp.zeros(dist.shape, F32)
        for window, dilation in patterns:
            count = count + ((dist % dilation == 0) & (dist <= window // 2)).astype(F32)
        return jnp.where(count > 0, jnp.log(jnp.maximum(count, 1.0)), NEG_INF)

    nb = s // A_QBLOCK
    shape1 = (nb, A_QBLOCK, A_LOCAL_WINDOW)
    q0 = lax.broadcasted_iota(jnp.int32, shape1, 0) * A_QBLOCK
    k0 = jnp.clip(q0 - (A_LOCAL_WINDOW - A_QBLOCK) // 2, 0, s - A_LOCAL_WINDOW)
    d1 = jnp.abs(q0 + lax.broadcasted_iota(jnp.int32, shape1, 1) - k0 - lax.broadcasted_iota(jnp.int32, shape1, 2))
    sub = s // A_STRIDE
    d2 = jnp.abs(lax.broadcasted_iota(jnp.int32, (sub, sub), 0) - lax.broadcasted_iota(jnp.int32, (sub, sub), 1)) * A_STRIDE
    return d1.astype(F32), log_count(d1, local), d2.astype(F32), log_count(d2, strided)


def _softmax_pv(s, v):
    m = jnp.max(s, axis=-1, keepdims=True)
    p = jnp.exp(s - m)
    l = jnp.sum(p, axis=-1, keepdims=True)
    o = jnp.dot(p.astype(BF16), v, preferred_element_type=F32)
    return o / l, m + jnp.log(l)


def _attn_a_kernel(slopes_ref, qn_ref, kn_ref, vn_ref, *refs):
    ns = A_STRIDE
    q4, k4, v4 = refs[0:ns], refs[ns:2 * ns], refs[2 * ns:3 * ns]
    d1_ref, l1_ref, d2_ref, l2_ref, o_ref, o2_scr, lse2_scr = refs[3 * ns:]
    slope = slopes_ref[pl.program_id(1)]
    s_len, hd = qn_ref.shape
    qb, win = A_QBLOCK, A_LOCAL_WINDOW
    nb = s_len // qb

    bias2 = l2_ref[...] - slope * d2_ref[...]
    for g in range(ns):
        scores = lax.dot_general(q4[g][...], k4[g][...], _NT, preferred_element_type=F32)
        out, lse = _softmax_pv(scores + bias2, v4[g][...])
        rows = pl.ds(g, s_len // ns, stride=ns)
        o2_scr[rows, :] = out
        lse2_scr[rows, :] = jnp.broadcast_to(lse, (s_len // ns, hd))

    for grp in range(nb // A_GROUP):
        blocks = range(grp * A_GROUP, (grp + 1) * A_GROUP)
        starts = [min(max(b * qb - (win - qb) // 2, 0), s_len - win) for b in blocks]
        q = qn_ref[grp * A_GROUP * qb:(grp + 1) * A_GROUP * qb, :].reshape(A_GROUP, qb, hd)
        k = jnp.stack([kn_ref[k0:k0 + win, :] for k0 in starts])
        v = jnp.stack([vn_ref[k0:k0 + win, :] for k0 in starts])
        rows = slice(grp * A_GROUP, (grp + 1) * A_GROUP)
        scores = jnp.einsum("bqd,bkd->bqk", q, k, preferred_element_type=F32)
        scores = scores + (l1_ref[rows] - slope * d1_ref[rows])
        m = jnp.max(scores, axis=-1, keepdims=True)
        p = jnp.exp(scores - m)
        l = jnp.sum(p, axis=-1, keepdims=True)
        out1 = jnp.einsum("bqk,bkd->bqd", p.astype(BF16), v, preferred_element_type=F32) / l
        lse1 = m + jnp.log(l)
        tok = slice(grp * A_GROUP * qb, (grp + 1) * A_GROUP * qb)
        out2 = o2_scr[tok, :].reshape(A_GROUP, qb, hd)
        lse2 = lse2_scr[tok, 0:1].reshape(A_GROUP, qb, 1)
        top = jnp.maximum(lse1, lse2)
        w1, w2 = jnp.exp(lse1 - top), jnp.exp(lse2 - top)
        out = (w1 * out1 + w2 * out2) / (w1 + w2)
        o_ref[tok, :] = out.reshape(A_GROUP * qb, hd).astype(o_ref.dtype)


def _attention_a(zqkv, tables, n_seq, s, n_heads):
    hd, ns = HEAD_DIM, A_STRIDE
    width = zqkv.shape[1]
    strided = zqkv.reshape(n_seq * s // ns, ns * width)
    cb = width // hd

    def natural(part):
        return pl.BlockSpec((s, hd), lambda b, h: (b, part * n_heads + h))

    def residue(part, g):
        return pl.BlockSpec((s // ns, hd), lambda b, h: (b, g * cb + part * n_heads + h))

    def whole(t):
        return pl.BlockSpec(t.shape, lambda b, h: (0,) * t.ndim)

    return pl.pallas_call(
        _attn_a_kernel,
        grid=(n_seq, n_heads),
        in_specs=[pl.BlockSpec(memory_space=pltpu.SMEM), natural(0), natural(1), natural(2)]
        + [residue(part, g) for part in range(3) for g in range(ns)]
        + [whole(t) for t in tables],
        out_specs=pl.BlockSpec((s, hd), lambda b, h: (b, h)),
        out_shape=jax.ShapeDtypeStruct((n_seq * s, n_heads * hd), BF16),
        scratch_shapes=[pltpu.VMEM((s, hd), F32), pltpu.VMEM((s, hd), F32)],
        compiler_params=_params("parallel", "parallel"),
        name="attn_a",
    )(_alibi_slopes(n_heads), zqkv, zqkv, zqkv, *([strided] * (3 * ns)), *tables)


def _attn_b_kernel(lam_init, slopes_ref, q_ref, k_ref, v_ref, dist_ref,
                   lq1_ref, lk1_ref, lq2_ref, lk2_ref, g_ref, o_ref):
    hd = HEAD_DIM
    bias = -slopes_ref[pl.program_id(2)] * dist_ref[...]
    lam = (jnp.exp(jnp.sum(lq1_ref[...] * lk1_ref[...], axis=-1, keepdims=True))
           - jnp.exp(jnp.sum(lq2_ref[...] * lk2_ref[...], axis=-1, keepdims=True)) + lam_init)

    def softmax_parts(q, k):
        s = lax.dot_general(q, k, _NT, preferred_element_type=F32) + bias
        p = jnp.exp(s - jnp.max(s, axis=-1, keepdims=True))
        return p, jnp.sum(p, axis=-1, keepdims=True)

    p1, l1 = softmax_parts(q_ref[:, :hd], k_ref[:, :hd])
    p2, l2 = softmax_parts(q_ref[:, hd:], k_ref[:, hd:])
    a = p1 * (1.0 / l1) - p2 * (lam / l2)
    o = jnp.dot(a.astype(BF16), v_ref[...], preferred_element_type=F32)
    ms = jnp.mean(o * o, axis=-1, keepdims=True)
    y = (o * lax.rsqrt(ms + NORM_EPS) * g_ref[...]) * (1.0 - lam_init)
    o_ref[...] = y.astype(o_ref.dtype)


def _attention_b(zqkv, dist, lam_vecs, g_subln, lam_init, n_seq, s, n_heads, col0, tq=512):
    tq = min(tq, s)
    nq = s // tq
    w = 2 * HEAD_DIM
    c0 = col0 // w
    vec = pl.BlockSpec((1, HEAD_DIM), lambda qi, b, h: (0, 0))
    return pl.pallas_call(
        functools.partial(_attn_b_kernel, lam_init),
        grid=(nq, n_seq, n_heads),
        in_specs=[
            pl.BlockSpec(memory_space=pltpu.SMEM),
            pl.BlockSpec((tq, w), lambda qi, b, h: (b * nq + qi, c0 + h)),
            pl.BlockSpec((s, w), lambda qi, b, h: (b, c0 + n_heads + h)),
            pl.BlockSpec((s, w), lambda qi, b, h: (b, c0 + 2 * n_heads + h)),
            pl.BlockSpec((tq, s), lambda qi, b, h: (qi, 0)),
            vec, vec, vec, vec,
            pl.BlockSpec((1, w), lambda qi, b, h: (0, 0)),
        ],
        out_specs=pl.BlockSpec((tq, w), lambda qi, b, h: (b * nq + qi, h)),
        out_shape=jax.ShapeDtypeStruct((n_seq * s, n_heads * w), BF16),
        compiler_params=_params("parallel", "parallel", "parallel"),
        name="attn_b",
    )(_alibi_slopes(n_heads), zqkv, zqkv, zqkv, dist,
      *[v.reshape(1, HEAD_DIM) for v in lam_vecs], g_subln.reshape(1, w))


def _merge_kernel(xn_ref, oa_ref, ob_ref, wga_ref, wgb_ref, wpa_ref, wpb_ref, o_ref):
    xn = xn_ref[...]
    ga = jnp.dot(xn, wga_ref[...], preferred_element_type=F32)
    gb = jnp.dot(xn, wgb_ref[...], preferred_element_type=F32)
    pa = jnp.dot(oa_ref[...], wpa_ref[...], preferred_element_type=F32)
    pb = jnp.dot(ob_ref[...], wpb_ref[...], preferred_element_type=F32)
    o_ref[...] = (jax.nn.sigmoid(ga) * pa + jax.nn.sigmoid(gb) * pb).astype(o_ref.dtype)


def _merge(xn, oa, ob, w_in, w_pa, w_pb, gate_col0, tm=512, tn=512):
    t, d = xn.shape
    tm, tn = min(tm, t), min(tn, d)
    ga0 = gate_col0 // tn
    gb0 = (gate_col0 + d) // tn
    return pl.pallas_call(
        _merge_kernel,
        grid=(t // tm, d // tn),
        in_specs=[
            pl.BlockSpec((tm, d), lambda i, j: (i, 0)),
            pl.BlockSpec((tm, oa.shape[1]), lambda i, j: (i, 0)),
            pl.BlockSpec((tm, ob.shape[1]), lambda i, j: (i, 0)),
            pl.BlockSpec((d, tn), lambda i, j: (0, ga0 + j)),
            pl.BlockSpec((d, tn), lambda i, j: (0, gb0 + j)),
            pl.BlockSpec((w_pa.shape[0], tn), lambda i, j: (0, j)),
            pl.BlockSpec((w_pb.shape[0], tn), lambda i, j: (0, j)),
        ],
        out_specs=pl.BlockSpec((tm, tn), lambda i, j: (i, j)),
        out_shape=jax.ShapeDtypeStruct((t, d), BF16),
        compiler_params=_params("parallel", "parallel"),
        name="merge",
    )(xn, oa, ob, w_in, w_in, w_pa, w_pb)


def _top_values(cur, n, row_iota):
    rows = cur.shape[0]
    out = []
    for _ in range(n):
        m = jnp.max(cur, axis=0, keepdims=True)
        out.append(m)
        first = jnp.min(jnp.where(cur == m, row_iota, rows), axis=0, keepdims=True)
        cur = jnp.where(row_iota == first, -jnp.inf, cur)
    return out


def _peer_score_kernel(q_ref, k1_ref, k2_ref, thr_ref, c_ref, s2_ref, e2_ref, v1_scr, v2_scr):
    n_heads, nk, c = k1_ref.shape
    tm = q_ref.shape[0]
    k = PK_TOPK
    iota_k = lax.broadcasted_iota(jnp.int32, (nk, tm), 0)
    pad = jnp.full((8 - 1, tm), -jnp.inf, F32)
    for h in range(n_heads):
        s1 = lax.dot_general(k1_ref[h], q_ref[:, 2 * h * c:(2 * h + 1) * c], _NT, preferred_element_type=F32)
        s2 = lax.dot_general(k2_ref[h], q_ref[:, (2 * h + 1) * c:(2 * h + 2) * c], _NT, preferred_element_type=F32)
        v1_scr[...] = jnp.concatenate(_top_values(s1, k + 1, iota_k) + [pad], axis=0)
        v2_scr[...] = jnp.concatenate(_top_values(s2, k + 1, iota_k) + [pad], axis=0)
        pieces = [v1_scr[0:1, :] + v2_scr[0:k, :]]
        for a in range(1, 8):
            pieces.append(v1_scr[a:a + 1, :] + v2_scr[0:8, :])
        pieces.append(v1_scr[8:k, :] + v2_scr[0:1, :])
        pieces.append(v1_scr[k:k + 8, :] + v2_scr[0:1, :])
        pieces.append(v1_scr[0:1, :] + v2_scr[k:k + 8, :])
        cand = jnp.concatenate(pieces, axis=0)
        iota_c = lax.broadcasted_iota(jnp.int32, cand.shape, 0)
        tops = _top_values(cand, k + 1, iota_c)
        tau = 0.5 * (tops[k - 1] + tops[k])
        z = jnp.exp(tops[0] - tops[0])
        for t in tops[1:k]:
            z = z + jnp.exp(t - tops[0])
        thr_ref[h] = tau - s1
        c_ref[h] = jnp.exp(s1 - v1_scr[0:1, :]) / z
        s2_ref[h] = s2
        e2_ref[h] = jnp.exp(s2 - v2_scr[0:1, :])


def _peer_scores(q, k1, k2, tm=512):
    t = q.shape[0]
    n_heads, nk, c = k1.shape
    tm = min(tm, t)
    out = jax.ShapeDtypeStruct((n_heads, nk, t), F32)
    ospec = pl.BlockSpec((n_heads, nk, tm), lambda i: (0, 0, i))
    kspec = pl.BlockSpec((n_heads, nk, c), lambda i: (0, 0, 0))
    return pl.pallas_call(
        _peer_score_kernel,
        grid=(t // tm,),
        in_specs=[pl.BlockSpec((tm, q.shape[1]), lambda i: (i, 0)), kspec, kspec],
        out_specs=[ospec] * 4,
        out_shape=[out] * 4,
        scratch_shapes=[pltpu.VMEM((PK_TOPK + 8, tm), F32), pltpu.VMEM((PK_TOPK + 8, tm), F32)],
        compiler_params=_params("parallel"),
        name="peer_scores",
    )(q, k1, k2)


def _peer_expert_kernel(xn_ref, u_ref, v_ref, thr_ref, c_ref, s2_ref, e2_ref, o_ref):
    j = pl.program_id(1)
    n_heads, nk, tm = s2_ref.shape
    te = u_ref.shape[0]
    @pl.when(j == 0)
    def _():
        o_ref[...] = jnp.zeros_like(o_ref)

    gates = []
    for r in range(te // nk):
        i1 = j * (te // nk) + r
        g_t = jnp.zeros((nk, tm), F32)
        for h in range(n_heads):
            thr = thr_ref[h, pl.ds(i1, 1), :]
            coef = c_ref[h, pl.ds(i1, 1), :]
            g_t = g_t + jnp.where(s2_ref[h] >= thr, e2_ref[h] * coef, 0.0)
        gates.append(g_t.T)
    hidden = lax.dot_general(xn_ref[...], u_ref[...], _NT, preferred_element_type=F32)
    a = (jax.nn.gelu(hidden) * jnp.concatenate(gates, axis=1)).astype(BF16)
    o_ref[...] += jnp.dot(a, v_ref[...], preferred_element_type=F32)


def _peer_experts(xn, u, v, thr, coef, s2, e2, tm=512, te=512):
    t, d = xn.shape
    n_exp = u.shape[0]
    n_heads, nk, _ = s2.shape
    tm, te = min(tm, t), min(te, n_exp)
    once = pl.Buffered(1)
    gspec = pl.BlockSpec((n_heads, nk, tm), lambda i, j: (0, 0, i), pipeline_mode=once)
    return pl.pallas_call(
        _peer_expert_kernel,
        grid=(t // tm, n_exp // te),
        in_specs=[
            pl.BlockSpec((tm, d), lambda i, j: (i, 0), pipeline_mode=once),
            pl.BlockSpec((te, d), lambda i, j: (j, 0)),
            pl.BlockSpec((te, d), lambda i, j: (j, 0)),
            gspec, gspec, gspec, gspec,
        ],
        out_specs=pl.BlockSpec((tm, d), lambda i, j: (i, 0)),
        out_shape=jax.ShapeDtypeStruct((t, d), F32),
        compiler_params=_params("parallel", "arbitrary"),
        name="peer_experts",
    )(xn, u, v, thr, coef, s2, e2)


def kernel(x_prompt, x_sample, w_in, w_proj_a, w_proj_b, w_out, g_mix_norm, lam_q1, lam_k1, lam_q2,
           lam_k2, g_subln, g_ffn_norm, w_query, sub_keys_1, sub_keys_2, expert_u, expert_v, g_final):
    s, d = x_prompt.shape[1], x_prompt.shape[2]
    assert x_sample.shape[1] == s and all(s % (w // 2) == 0 for w, _ in DILATED_PATTERNS)
    n_seq = x_prompt.shape[0] + x_sample.shape[0]
    depth = w_in.shape[0]
    a_width, b_width = w_proj_a.shape[1], w_proj_b.shape[1]
    a_heads, b_heads = a_width // HEAD_DIM, b_width // (2 * HEAD_DIM)
    qkv_width = 3 * a_width + 3 * b_width
    scale = HEAD_DIM ** -0.5

    x_chunks = [x_prompt.reshape(-1, d), x_sample.reshape(-1, d)]
    out_rows = [c.shape[0] for c in x_chunks]
    dist = _distance_table(s)
    tables_a = _attn_a_tables(s)
    one_a, one_b = jnp.ones((2 * a_width,), F32), jnp.ones((2 * b_width,), F32)
    colscale = jnp.concatenate([jnp.full((a_width,), scale, F32), one_a, jnp.full((b_width,), scale, F32), one_b])

    for l in range(depth):
        lam_init = 0.8 - 0.6 * math.exp(-0.3 * l)
        w_in_l = w_in[l].astype(BF16)
        xn = _rmsnorm(x_chunks, g_mix_norm[l], BF16)
        zqkv = _matmul(xn, w_in_l, BF16, n_cols=qkv_width, colscale=colscale, name="in_proj")
        o_a = _attention_a(zqkv, tables_a, n_seq, s, a_heads)
        o_b = _attention_b(zqkv, dist, (lam_q1[l], lam_k1[l], lam_q2[l], lam_k2[l]), g_subln[l], lam_init,
                           n_seq, s, b_heads, 3 * a_width)
        merged = _merge(xn, o_a, o_b, w_in_l, w_proj_a[l].astype(BF16), w_proj_b[l].astype(BF16), qkv_width)
        x1 = _matmul(merged, w_out[l].astype(BF16), F32, residual=x_chunks, name="out_proj")
        xn2 = _rmsnorm([x1], g_ffn_norm[l], BF16)
        q = _matmul(xn2, w_query[l].astype(BF16), BF16, name="peer_query")
        thr, coef, s2, e2 = _peer_scores(q, sub_keys_1[l].astype(BF16), sub_keys_2[l].astype(BF16))
        peer = _peer_experts(xn2, expert_u[l].astype(BF16), expert_v[l].astype(BF16), thr, coef, s2, e2)
        if l + 1 < depth:
            x_chunks = [x1 + peer]
    y_prompt, y_sample = _add_rmsnorm_split(x1, peer, g_final, out_rows, F32)
    return y_prompt.reshape(x_prompt.shape), y_sample.reshape(x_sample.shape)
```

```python
import functools
import math

import numpy as np
import jax
import jax.numpy as jnp
from jax import lax
from jax.experimental import pallas as pl
from jax.experimental.pallas import tpu as pltpu

HEAD_DIM = 128
DILATED_PATTERNS = ((128, 1), (512, 4), (2048, 16))
PK_TOPK = 16
NORM_EPS = 1e-6
NEG_INF = -1e30
MIB = 1024 * 1024
VMEM_LIMIT_BYTES = 56 * MIB

F32 = jnp.float32
BF16 = jnp.bfloat16
_NT = (((1,), (1,)), ((), ()))


def _params(*sem):
    return pltpu.CompilerParams(dimension_semantics=sem, vmem_limit_bytes=VMEM_LIMIT_BYTES)


def _chunk_specs(chunks, tm, tn, row_col):
    specs, bounds, lo = [], [], 0
    for rows in chunks:
        n = rows // tm

        def index_map(*g, lo=lo, n=n):
            i, j = row_col(*g)
            return (jnp.clip(i - lo, 0, n - 1), j)

        specs.append(pl.BlockSpec((tm, tn), index_map))
        bounds.append((lo, lo + n))
        lo += n
    return specs, bounds


def _for_active_chunk(i, refs, bounds, body):
    if len(refs) == 1:
        body(refs[0])
        return
    for ref, (lo, hi) in zip(refs, bounds):
        pl.when((i >= lo) & (i < hi))(functools.partial(body, ref))


def _rms(x, g):
    ms = jnp.mean(x * x, axis=-1, keepdims=True)
    return x * lax.rsqrt(ms + NORM_EPS) * g


def _rmsnorm_kernel(bounds, *refs):
    *x_refs, g_ref, o_ref = refs

    def body(x_ref):
        o_ref[...] = _rms(x_ref[...], g_ref[...]).astype(o_ref.dtype)

    _for_active_chunk(pl.program_id(0), x_refs, bounds, body)


def _rmsnorm(chunks, g, out_dtype, tm=256):
    d = chunks[0].shape[1]
    t = sum(c.shape[0] for c in chunks)
    specs, bounds = _chunk_specs([c.shape[0] for c in chunks], tm, d, lambda i: (i, 0))
    return pl.pallas_call(
        functools.partial(_rmsnorm_kernel, bounds),
        grid=(t // tm,),
        in_specs=specs + [pl.BlockSpec((1, d), lambda i: (0, 0))],
        out_specs=pl.BlockSpec((tm, d), lambda i: (i, 0)),
        out_shape=jax.ShapeDtypeStruct((t, d), out_dtype),
        compiler_params=_params("parallel"),
        name="rmsnorm",
    )(*chunks, g.reshape(1, d))


def _add_rmsnorm_kernel(bounds, a_ref, b_ref, g_ref, *o_refs):
    y = _rms(a_ref[...] + b_ref[...], g_ref[...])

    def body(o_ref):
        o_ref[...] = y.astype(o_ref.dtype)

    _for_active_chunk(pl.program_id(0), o_refs, bounds, body)


def _add_rmsnorm_split(a, b, g, out_rows, out_dtype, tm=256):
    t, d = a.shape
    row = pl.BlockSpec((tm, d), lambda i: (i, 0))
    specs, bounds = _chunk_specs(out_rows, tm, d, lambda i: (i, 0))
    return pl.pallas_call(
        functools.partial(_add_rmsnorm_kernel, bounds),
        grid=(t // tm,),
        in_specs=[row, row, pl.BlockSpec((1, d), lambda i: (0, 0))],
        out_specs=specs,
        out_shape=[jax.ShapeDtypeStruct((r, d), out_dtype) for r in out_rows],
        compiler_params=_params("arbitrary"),
        name="add_rmsnorm",
    )(a, b, g.reshape(1, d))


def _mm_kernel(a_ref, b_ref, o_ref):
    o_ref[...] = jnp.dot(a_ref[...], b_ref[...], preferred_element_type=F32).astype(o_ref.dtype)


def _mm_colscale_kernel(a_ref, b_ref, cs_ref, o_ref):
    acc = jnp.dot(a_ref[...], b_ref[...], preferred_element_type=F32)
    o_ref[...] = (acc * cs_ref[...]).astype(o_ref.dtype)


def _mm_residual_kernel(bounds, a_ref, b_ref, *refs):
    *r_refs, o_ref = refs
    acc = jnp.dot(a_ref[...], b_ref[...], preferred_element_type=F32)

    def body(r_ref):
        o_ref[...] = (r_ref[...] + acc).astype(o_ref.dtype)

    _for_active_chunk(pl.program_id(1), r_refs, bounds, body)


def _matmul(a, b, out_dtype, *, n_cols=None, colscale=None, residual=None, tm=512, tn=1024, name="matmul"):
    m, k = a.shape
    n = b.shape[1] if n_cols is None else n_cols
    tm, tn = min(tm, m), min(tn, n)
    a_spec = pl.BlockSpec((tm, k), lambda j, i: (i, 0))
    b_spec = pl.BlockSpec((k, tn), lambda j, i: (0, j))
    o_spec = pl.BlockSpec((tm, tn), lambda j, i: (i, j))
    if colscale is not None:
        kern, ins = _mm_colscale_kernel, (a, b, colscale.reshape(1, n))
        specs = [a_spec, b_spec, pl.BlockSpec((1, tn), lambda j, i: (0, j))]
    elif residual is not None:
        r_specs, bounds = _chunk_specs([r.shape[0] for r in residual], tm, tn, lambda j, i: (i, j))
        kern, ins, specs = functools.partial(_mm_residual_kernel, bounds), (a, b, *residual), [a_spec, b_spec] + r_specs
    else:
        kern, ins, specs = _mm_kernel, (a, b), [a_spec, b_spec]
    return pl.pallas_call(
        kern,
        grid=(n // tn, m // tm),
        in_specs=specs,
        out_specs=o_spec,
        out_shape=jax.ShapeDtypeStruct((m, n), out_dtype),
        compiler_params=_params("parallel", "parallel"),
        name=name,
    )(*ins)


def _distance_table(s):
    i = lax.broadcasted_iota(jnp.int32, (s, s), 0)
    j = lax.broadcasted_iota(jnp.int32, (s, s), 1)
    return jnp.abs(i - j).astype(F32)


def _alibi_slopes(n):
    return jnp.asarray(2.0 ** (-8.0 * np.arange(1, n + 1) / n), dtype=F32)


A_STRIDE = 4
A_QBLOCK = 128
A_LOCAL_WINDOW = 2 * A_QBLOCK
A_GROUP = 4


def _attn_a_tables(s):
    local = [(w, d) for w, d in DILATED_PATTERNS if d == 1]
    strided = [(w, d) for w, d in DILATED_PATTERNS if d != 1]
    assert all(d % A_STRIDE == 0 for _, d in strided) and all(w // 2 <= A_QBLOCK // 2 for w, _ in local)

    def log_count(dist, patterns):
        count = jnp.zeros(dist.shape, F32)
        for window, dilation in patterns:
            count = count + ((dist % dilation == 0) & (dist <= window // 2)).astype(F32)
        return jnp.where(count > 0, jnp.log(jnp.maximum(count, 1.0)), NEG_INF)

    nb = s // A_QBLOCK
    shape1 = (nb, A_QBLOCK, A_LOCAL_WINDOW)
    q0 = lax.broadcasted_iota(jnp.int32, shape1, 0) * A_QBLOCK
    k0 = jnp.clip(q0 - (A_LOCAL_WINDOW - A_QBLOCK) // 2, 0, s - A_LOCAL_WINDOW)
    d1 = jnp.abs(q0 + lax.broadcasted_iota(jnp.int32, shape1, 1) - k0 - lax.broadcasted_iota(jnp.int32, shape1, 2))
    sub = s // A_STRIDE
    d2 = jnp.abs(lax.broadcasted_iota(jnp.int32, (sub, sub), 0) - lax.broadcasted_iota(jnp.int32, (sub, sub), 1)) * A_STRIDE
    return d1.astype(F32), log_count(d1, local), d2.astype(F32), log_count(d2, strided)


def _softmax_pv(s, v):
    m = jnp.max(s, axis=-1, keepdims=True)
    p = jnp.exp(s - m)
    l = jnp.sum(p, axis=-1, keepdims=True)
    o = jnp.dot(p.astype(BF16), v, preferred_element_type=F32)
    return o / l, m + jnp.log(l)


def _attn_a_kernel(slopes_ref, qn_ref, kn_ref, vn_ref, d1_ref, l1_ref, d2_ref, l2_ref, o_ref, o2_scr, lse2_scr, wide_scr):
    ns = A_STRIDE
    for part, ref in enumerate((qn_ref, kn_ref, vn_ref)):
        wide_scr[part] = ref[...].astype(F32)
    slope = slopes_ref[pl.program_id(1)]
    s_len, hd = qn_ref.shape
    qb, win = A_QBLOCK, A_LOCAL_WINDOW
    nb = s_len // qb

    bias2 = l2_ref[...] - slope * d2_ref[...]
    for g in range(ns):
        rows = pl.ds(g, s_len // ns, stride=ns)
        q, k, v = (wide_scr[part, rows, :].astype(BF16) for part in range(3))
        scores = lax.dot_general(q, k, _NT, preferred_element_type=F32)
        out, lse = _softmax_pv(scores + bias2, v)
        o2_scr[rows, :] = out
        lse2_scr[rows, :] = jnp.broadcast_to(lse, (s_len // ns, hd))

    for grp in range(nb // A_GROUP):
        blocks = range(grp * A_GROUP, (grp + 1) * A_GROUP)
        starts = [min(max(b * qb - (win - qb) // 2, 0), s_len - win) for b in blocks]
        q = qn_ref[grp * A_GROUP * qb:(grp + 1) * A_GROUP * qb, :].reshape(A_GROUP, qb, hd)
        k = jnp.stack([kn_ref[k0:k0 + win, :] for k0 in starts])
        v = jnp.stack([vn_ref[k0:k0 + win, :] for k0 in starts])
        rows = slice(grp * A_GROUP, (grp + 1) * A_GROUP)
        scores = jnp.einsum("bqd,bkd->bqk", q, k, preferred_element_type=F32)
        scores = scores + (l1_ref[rows] - slope * d1_ref[rows])
        m = jnp.max(scores, axis=-1, keepdims=True)
        p = jnp.exp(scores - m)
        l = jnp.sum(p, axis=-1, keepdims=True)
        out1 = jnp.einsum("bqk,bkd->bqd", p.astype(BF16), v, preferred_element_type=F32) / l
        lse1 = m + jnp.log(l)
        tok = slice(grp * A_GROUP * qb, (grp + 1) * A_GROUP * qb)
        out2 = o2_scr[tok, :].reshape(A_GROUP, qb, hd)
        lse2 = lse2_scr[tok, 0:1].reshape(A_GROUP, qb, 1)
        top = jnp.maximum(lse1, lse2)
        w1, w2 = jnp.exp(lse1 - top), jnp.exp(lse2 - top)
        out = (w1 * out1 + w2 * out2) / (w1 + w2)
        o_ref[tok, :] = out.reshape(A_GROUP * qb, hd).astype(o_ref.dtype)


def _attention_a(zqkv, tables, n_seq, s, n_heads):
    hd = HEAD_DIM

    def natural(part):
        return pl.BlockSpec((s, hd), lambda b, h: (b, part * n_heads + h))

    def whole(t):
        return pl.BlockSpec(t.shape, lambda b, h: (0,) * t.ndim)

    return pl.pallas_call(
        _attn_a_kernel,
        grid=(n_seq, n_heads),
        in_specs=[pl.BlockSpec(memory_space=pltpu.SMEM), natural(0), natural(1), natural(2)] + [whole(t) for t in tables],
        out_specs=pl.BlockSpec((s, hd), lambda b, h: (b, h)),
        out_shape=jax.ShapeDtypeStruct((n_seq * s, n_heads * hd), BF16),
        scratch_shapes=[pltpu.VMEM((s, hd), F32), pltpu.VMEM((s, hd), F32), pltpu.VMEM((3, s, hd), F32)],
        compiler_params=_params("parallel", "parallel"),
        name="attn_a",
    )(_alibi_slopes(n_heads), zqkv, zqkv, zqkv, *tables)


def _attn_b_kernel(lam_init, slopes_ref, q_ref, k_ref, v_ref, dist_ref,
                   lq1_ref, lk1_ref, lq2_ref, lk2_ref, g_ref, o_ref):
    hd = HEAD_DIM
    bias = -slopes_ref[pl.program_id(2)] * dist_ref[...]
    lam = (jnp.exp(jnp.sum(lq1_ref[...] * lk1_ref[...], axis=-1, keepdims=True))
           - jnp.exp(jnp.sum(lq2_ref[...] * lk2_ref[...], axis=-1, keepdims=True)) + lam_init)

    def softmax_parts(q, k):
        s = lax.dot_general(q, k, _NT, preferred_element_type=F32) + bias
        p = jnp.exp(s - jnp.max(s, axis=-1, keepdims=True))
        return p, jnp.sum(p, axis=-1, keepdims=True)

    p1, l1 = softmax_parts(q_ref[:, :hd], k_ref[:, :hd])
    p2, l2 = softmax_parts(q_ref[:, hd:], k_ref[:, hd:])
    a = p1 * (1.0 / l1) - p2 * (lam / l2)
    o = jnp.dot(a.astype(BF16), v_ref[...], preferred_element_type=F32)
    ms = jnp.mean(o * o, axis=-1, keepdims=True)
    y = (o * lax.rsqrt(ms + NORM_EPS) * g_ref[...]) * (1.0 - lam_init)
    o_ref[...] = y.astype(o_ref.dtype)


def _attention_b(zqkv, dist, lam_vecs, g_subln, lam_init, n_seq, s, n_heads, col0, tq=512):
    tq = min(tq, s)
    nq = s // tq
    w = 2 * HEAD_DIM
    c0 = col0 // w
    vec = pl.BlockSpec((1, HEAD_DIM), lambda qi, b, h: (0, 0))
    return pl.pallas_call(
        functools.partial(_attn_b_kernel, lam_init),
        grid=(nq, n_seq, n_heads),
        in_specs=[
            pl.BlockSpec(memory_space=pltpu.SMEM),
            pl.BlockSpec((tq, w), lambda qi, b, h: (b * nq + qi, c0 + h)),
            pl.BlockSpec((s, w), lambda qi, b, h: (b, c0 + n_heads + h)),
            pl.BlockSpec((s, w), lambda qi, b, h: (b, c0 + 2 * n_heads + h)),
            pl.BlockSpec((tq, s), lambda qi, b, h: (qi, 0)),
            vec, vec, vec, vec,
            pl.BlockSpec((1, w), lambda qi, b, h: (0, 0)),
        ],
        out_specs=pl.BlockSpec((tq, w), lambda qi, b, h: (b * nq + qi, h)),
        out_shape=jax.ShapeDtypeStruct((n_seq * s, n_heads * w), BF16),
        compiler_params=_params("parallel", "parallel", "parallel"),
        name="attn_b",
    )(_alibi_slopes(n_heads), zqkv, zqkv, zqkv, dist,
      *[v.reshape(1, HEAD_DIM) for v in lam_vecs], g_subln.reshape(1, w))


def _merge_kernel(xn_ref, oa_ref, ob_ref, wga_ref, wgb_ref, wpa_ref, wpb_ref, o_ref):
    xn = xn_ref[...]
    ga = jnp.dot(xn, wga_ref[...], preferred_element_type=F32)
    gb = jnp.dot(xn, wgb_ref[...], preferred_element_type=F32)
    pa = jnp.dot(oa_ref[...], wpa_ref[...], preferred_element_type=F32)
    pb = jnp.dot(ob_ref[...], wpb_ref[...], preferred_element_type=F32)
    o_ref[...] = (jax.nn.sigmoid(ga) * pa + jax.nn.sigmoid(gb) * pb).astype(o_ref.dtype)


def _merge(xn, oa, ob, w_in, w_pa, w_pb, gate_col0, tm=512, tn=512):
    t, d = xn.shape
    tm, tn = min(tm, t), min(tn, d)
    ga0 = gate_col0 // tn
    gb0 = (gate_col0 + d) // tn
    return pl.pallas_call(
        _merge_kernel,
        grid=(t // tm, d // tn),
        in_specs=[
            pl.BlockSpec((tm, d), lambda i, j: (i, 0)),
            pl.BlockSpec((tm, oa.shape[1]), lambda i, j: (i, 0)),
            pl.BlockSpec((tm, ob.shape[1]), lambda i, j: (i, 0)),
            pl.BlockSpec((d, tn), lambda i, j: (0, ga0 + j)),
            pl.BlockSpec((d, tn), lambda i, j: (0, gb0 + j)),
            pl.BlockSpec((w_pa.shape[0], tn), lambda i, j: (0, j)),
            pl.BlockSpec((w_pb.shape[0], tn), lambda i, j: (0, j)),
        ],
        out_specs=pl.BlockSpec((tm, tn), lambda i, j: (i, j)),
        out_shape=jax.ShapeDtypeStruct((t, d), BF16),
        compiler_params=_params("parallel", "parallel"),
        name="merge",
    )(xn, oa, ob, w_in, w_in, w_pa, w_pb)


def _top_values(cur, n, row_iota):
    rows = cur.shape[0]
    out = []
    for _ in range(n):
        m = jnp.max(cur, axis=0, keepdims=True)
        out.append(m)
        first = jnp.min(jnp.where(cur == m, row_iota, rows), axis=0, keepdims=True)
        cur = jnp.where(row_iota == first, -jnp.inf, cur)
    return out


def _peer_score_kernel(q_ref, k1_ref, k2_ref, thr_ref, c_ref, s2_ref, e2_ref, v1_scr, v2_scr):
    n_heads, nk, c = k1_ref.shape
    tm = q_ref.shape[0]
    k = PK_TOPK
    iota_k = lax.broadcasted_iota(jnp.int32, (nk, tm), 0)
    pad = jnp.full((8 - 1, tm), -jnp.inf, F32)
    for h in range(n_heads):
        s1 = lax.dot_general(k1_ref[h], q_ref[:, 2 * h * c:(2 * h + 1) * c], _NT, preferred_element_type=F32)
        s2 = lax.dot_general(k2_ref[h], q_ref[:, (2 * h + 1) * c:(2 * h + 2) * c], _NT, preferred_element_type=F32)
        v1_scr[...] = jnp.concatenate(_top_values(s1, k + 1, iota_k) + [pad], axis=0)
        v2_scr[...] = jnp.concatenate(_top_values(s2, k + 1, iota_k) + [pad], axis=0)
        pieces = [v1_scr[0:1, :] + v2_scr[0:k, :]]
        for a in range(1, 8):
            pieces.append(v1_scr[a:a + 1, :] + v2_scr[0:8, :])
        pieces.append(v1_scr[8:k, :] + v2_scr[0:1, :])
        pieces.append(v1_scr[k:k + 8, :] + v2_scr[0:1, :])
        pieces.append(v1_scr[0:1, :] + v2_scr[k:k + 8, :])
        cand = jnp.concatenate(pieces, axis=0)
        iota_c = lax.broadcasted_iota(jnp.int32, cand.shape, 0)
        tops = _top_values(cand, k + 1, iota_c)
        tau = 0.5 * (tops[k - 1] + tops[k])
        z = jnp.exp(tops[0] - tops[0])
        for t in tops[1:k]:
            z = z + jnp.exp(t - tops[0])
        thr_ref[h] = tau - s1
        c_ref[h] = jnp.exp(s1 - v1_scr[0:1, :]) / z
        s2_ref[h] = s2
        e2_ref[h] = jnp.exp(s2 - v2_scr[0:1, :])


def _peer_scores(q, k1, k2, tm=512):
    t = q.shape[0]
    n_heads, nk, c = k1.shape
    tm = min(tm, t)
    out = jax.ShapeDtypeStruct((n_heads, nk, t), F32)
    ospec = pl.BlockSpec((n_heads, nk, tm), lambda i: (0, 0, i))
    kspec = pl.BlockSpec((n_heads, nk, c), lambda i: (0, 0, 0))
    return pl.pallas_call(
        _peer_score_kernel,
        grid=(t // tm,),
        in_specs=[pl.BlockSpec((tm, q.shape[1]), lambda i: (i, 0)), kspec, kspec],
        out_specs=[ospec] * 4,
        out_shape=[out] * 4,
        scratch_shapes=[pltpu.VMEM((PK_TOPK + 8, tm), F32), pltpu.VMEM((PK_TOPK + 8, tm), F32)],
        compiler_params=_params("parallel"),
        name="peer_scores",
    )(q, k1, k2)


def _peer_expert_kernel(xn_ref, u_ref, v_ref, thr_ref, c_ref, s2_ref, e2_ref, o_ref):
    j = pl.program_id(1)
    n_heads, nk, tm = s2_ref.shape
    te = u_ref.shape[0]
    @pl.when(j == 0)
    def _():
        o_ref[...] = jnp.zeros_like(o_ref)

    gates = []
    for r in range(te // nk):
        i1 = j * (te // nk) + r
        g_t = jnp.zeros((nk, tm), F32)
        for h in range(n_heads):
            thr = thr_ref[h, pl.ds(i1, 1), :]
            coef = c_ref[h, pl.ds(i1, 1), :]
            g_t = g_t + jnp.where(s2_ref[h] >= thr, e2_ref[h] * coef, 0.0)
        gates.append(g_t.T)
    hidden = lax.dot_general(xn_ref[...], u_ref[...], _NT, preferred_element_type=F32)
    a = (jax.nn.gelu(hidden) * jnp.concatenate(gates, axis=1)).astype(BF16)
    o_ref[...] += jnp.dot(a, v_ref[...], preferred_element_type=F32)


def _peer_experts(xn, u, v, thr, coef, s2, e2, tm=512, te=512):
    t, d = xn.shape
    n_exp = u.shape[0]
    n_heads, nk, _ = s2.shape
    tm, te = min(tm, t), min(te, n_exp)
    once = pl.Buffered(1)
    gspec = pl.BlockSpec((n_heads, nk, tm), lambda i, j: (0, 0, i), pipeline_mode=once)
    return pl.pallas_call(
        _peer_expert_kernel,
        grid=(t // tm, n_exp // te),
        in_specs=[
            pl.BlockSpec((tm, d), lambda i, j: (i, 0), pipeline_mode=once),
            pl.BlockSpec((te, d), lambda i, j: (j, 0)),
            pl.BlockSpec((te, d), lambda i, j: (j, 0)),
            gspec, gspec, gspec, gspec,
        ],
        out_specs=pl.BlockSpec((tm, d), lambda i, j: (i, 0)),
        out_shape=jax.ShapeDtypeStruct((t, d), F32),
        compiler_params=_params("parallel", "arbitrary"),
        name="peer_experts",
    )(xn, u, v, thr, coef, s2, e2)


def kernel(x_prompt, x_sample, w_in, w_proj_a, w_proj_b, w_out, g_mix_norm, lam_q1, lam_k1, lam_q2,
           lam_k2, g_subln, g_ffn_norm, w_query, sub_keys_1, sub_keys_2, expert_u, expert_v, g_final):
    s, d = x_prompt.shape[1], x_prompt.shape[2]
    assert x_sample.shape[1] == s and all(s % (w // 2) == 0 for w, _ in DILATED_PATTERNS)
    n_seq = x_prompt.shape[0] + x_sample.shape[0]
    depth = w_in.shape[0]
    a_width, b_width = w_proj_a.shape[1], w_proj_b.shape[1]
    a_heads, b_heads = a_width // HEAD_DIM, b_width // (2 * HEAD_DIM)
    qkv_width = 3 * a_width + 3 * b_width
    scale = HEAD_DIM ** -0.5

    x_chunks = [x_prompt.reshape(-1, d), x_sample.reshape(-1, d)]
    out_rows = [c.shape[0] for c in x_chunks]
    dist = _distance_table(s)
    tables_a = _attn_a_tables(s)
    one_a, one_b = jnp.ones((2 * a_width,), F32), jnp.ones((2 * b_width,), F32)
    colscale = jnp.concatenate([jnp.full((a_width,), scale, F32), one_a, jnp.full((b_width,), scale, F32), one_b])

    for l in range(depth):
        lam_init = 0.8 - 0.6 * math.exp(-0.3 * l)
        w_in_l = w_in[l].astype(BF16)
        xn = _rmsnorm(x_chunks, g_mix_norm[l], BF16)
        zqkv = _matmul(xn, w_in_l, BF16, n_cols=qkv_width, colscale=colscale, name="in_proj")
        o_a = _attention_a(zqkv, tables_a, n_seq, s, a_heads)
        o_b = _attention_b(zqkv, dist, (lam_q1[l], lam_k1[l], lam_q2[l], lam_k2[l]), g_subln[l], lam_init,
                           n_seq, s, b_heads, 3 * a_width)
        merged = _merge(xn, o_a, o_b, w_in_l, w_proj_a[l].astype(BF16), w_proj_b[l].astype(BF16), qkv_width)
        x1 = _matmul(merged, w_out[l].astype(BF16), F32, residual=x_chunks, name="out_proj")
        xn2 = _rmsnorm([x1], g_ffn_norm[l], BF16)
        q = _matmul(xn2, w_query[l].astype(BF16), BF16, name="peer_query")
        thr, coef, s2, e2 = _peer_scores(q, sub_keys_1[l].astype(BF16), sub_keys_2[l].astype(BF16))
        peer = _peer_experts(xn2, expert_u[l].astype(BF16), expert_v[l].astype(BF16), thr, coef, s2, e2)
        if l + 1 < depth:
            x_chunks = [x1 + peer]
    y_prompt, y_sample = _add_rmsnorm_split(x1, peer, g_final, out_rows, F32)
    return y_prompt.reshape(x_prompt.shape), y_sample.reshape(x_sample.shape)
```

```python
import functools
import math

import numpy as np
import jax
import jax.numpy as jnp
from jax import lax
from jax.experimental import pallas as pl
from jax.experimental.pallas import tpu as pltpu

HEAD_DIM = 128
DILATED_PATTERNS = ((128, 1), (512, 4), (2048, 16))
PK_TOPK = 16
NORM_EPS = 1e-6
NEG_INF = -1e30
MIB = 1024 * 1024
VMEM_LIMIT_BYTES = 56 * MIB

F32 = jnp.float32
BF16 = jnp.bfloat16
_NT = (((1,), (1,)), ((), ()))


def _params(*sem):
    return pltpu.CompilerParams(dimension_semantics=sem, vmem_limit_bytes=VMEM_LIMIT_BYTES)


def _chunk_specs(chunks, tm, tn, row_col):
    specs, bounds, lo = [], [], 0
    for rows in chunks:
        n = rows // tm

        def index_map(*g, lo=lo, n=n):
            i, j = row_col(*g)
            return (jnp.clip(i - lo, 0, n - 1), j)

        specs.append(pl.BlockSpec((tm, tn), index_map))
        bounds.append((lo, lo + n))
        lo += n
    return specs, bounds


def _for_active_chunk(i, refs, bounds, body):
    if len(refs) == 1:
        body(refs[0])
        return
    for ref, (lo, hi) in zip(refs, bounds):
        pl.when((i >= lo) & (i < hi))(functools.partial(body, ref))


def _rms(x, g):
    ms = jnp.mean(x * x, axis=-1, keepdims=True)
    return x * lax.rsqrt(ms + NORM_EPS) * g


def _rmsnorm_kernel(bounds, *refs):
    *x_refs, g_ref, o_ref = refs

    def body(x_ref):
        o_ref[...] = _rms(x_ref[...], g_ref[...]).astype(o_ref.dtype)

    _for_active_chunk(pl.program_id(0), x_refs, bounds, body)


def _rmsnorm(chunks, g, out_dtype, tm=256):
    d = chunks[0].shape[1]
    t = sum(c.shape[0] for c in chunks)
    specs, bounds = _chunk_specs([c.shape[0] for c in chunks], tm, d, lambda i: (i, 0))
    return pl.pallas_call(
        functools.partial(_rmsnorm_kernel, bounds),
        grid=(t // tm,),
        in_specs=specs + [pl.BlockSpec((1, d), lambda i: (0, 0))],
        out_specs=pl.BlockSpec((tm, d), lambda i: (i, 0)),
        out_shape=jax.ShapeDtypeStruct((t, d), out_dtype),
        compiler_params=_params("parallel"),
        name="rmsnorm",
    )(*chunks, g.reshape(1, d))


def _add_rmsnorm_kernel(bounds, a_ref, b_ref, g_ref, *o_refs):
    y = _rms(a_ref[...] + b_ref[...], g_ref[...])

    def body(o_ref):
        o_ref[...] = y.astype(o_ref.dtype)

    _for_active_chunk(pl.program_id(0), o_refs, bounds, body)


def _add_rmsnorm_split(a, b, g, out_rows, out_dtype, tm=256):
    t, d = a.shape
    row = pl.BlockSpec((tm, d), lambda i: (i, 0))
    specs, bounds = _chunk_specs(out_rows, tm, d, lambda i: (i, 0))
    return pl.pallas_call(
        functools.partial(_add_rmsnorm_kernel, bounds),
        grid=(t // tm,),
        in_specs=[row, row, pl.BlockSpec((1, d), lambda i: (0, 0))],
        out_specs=specs,
        out_shape=[jax.ShapeDtypeStruct((r, d), out_dtype) for r in out_rows],
        compiler_params=_params("arbitrary"),
        name="add_rmsnorm",
    )(a, b, g.reshape(1, d))


def _mm_kernel(a_ref, b_ref, o_ref):
    o_ref[...] = jnp.dot(a_ref[...], b_ref[...], preferred_element_type=F32).astype(o_ref.dtype)


def _mm_colscale_kernel(a_ref, b_ref, cs_ref, o_ref):
    acc = jnp.dot(a_ref[...], b_ref[...], preferred_element_type=F32)
    o_ref[...] = (acc * cs_ref[...]).astype(o_ref.dtype)


def _mm_residual_kernel(bounds, a_ref, b_ref, *refs):
    *r_refs, o_ref = refs
    acc = jnp.dot(a_ref[...], b_ref[...], preferred_element_type=F32)

    def body(r_ref):
        o_ref[...] = (r_ref[...] + acc).astype(o_ref.dtype)

    _for_active_chunk(pl.program_id(1), r_refs, bounds, body)


def _matmul(a, b, out_dtype, *, n_cols=None, colscale=None, residual=None, tm=512, tn=1024, name="matmul"):
    m, k = a.shape
    n = b.shape[1] if n_cols is None else n_cols
    tm, tn = min(tm, m), min(tn, n)
    a_spec = pl.BlockSpec((tm, k), lambda j, i: (i, 0))
    b_spec = pl.BlockSpec((k, tn), lambda j, i: (0, j))
    o_spec = pl.BlockSpec((tm, tn), lambda j, i: (i, j))
    if colscale is not None:
        kern, ins = _mm_colscale_kernel, (a, b, colscale.reshape(1, n))
        specs = [a_spec, b_spec, pl.BlockSpec((1, tn), lambda j, i: (0, j))]
    elif residual is not None:
        r_specs, bounds = _chunk_specs([r.shape[0] for r in residual], tm, tn, lambda j, i: (i, j))
        kern, ins, specs = functools.partial(_mm_residual_kernel, bounds), (a, b, *residual), [a_spec, b_spec] + r_specs
    else:
        kern, ins, specs = _mm_kernel, (a, b), [a_spec, b_spec]
    return pl.pallas_call(
        kern,
        grid=(n // tn, m // tm),
        in_specs=specs,
        out_specs=o_spec,
        out_shape=jax.ShapeDtypeStruct((m, n), out_dtype),
        compiler_params=_params("parallel", "parallel"),
        name=name,
    )(*ins)


def _distance_table(s):
    i = lax.broadcasted_iota(jnp.int32, (s, s), 0)
    j = lax.broadcasted_iota(jnp.int32, (s, s), 1)
    return jnp.abs(i - j).astype(F32)


def _alibi_slopes(n):
    return jnp.asarray(2.0 ** (-8.0 * np.arange(1, n + 1) / n), dtype=F32)


A_STRIDE = 4
A_QBLOCK = 128
A_LOCAL_WINDOW = 2 * A_QBLOCK
A_GROUP = 4


def _attn_a_tables(s):
    local = [(w, d) for w, d in DILATED_PATTERNS if d == 1]
    strided = [(w, d) for w, d in DILATED_PATTERNS if d != 1]
    assert all(d % A_STRIDE == 0 for _, d in strided) and all(w // 2 <= A_QBLOCK // 2 for w, _ in local)

    def log_count(dist, patterns):
        count = jnp.zeros(dist.shape, F32)
        for window, dilation in patterns:
            count = count + ((dist % dilation == 0) & (dist <= window // 2)).astype(F32)
        return jnp.where(count > 0, jnp.log(jnp.maximum(count, 1.0)), NEG_INF)

    nb = s // A_QBLOCK
    shape1 = (nb, A_QBLOCK, A_LOCAL_WINDOW)
    q0 = lax.broadcasted_iota(jnp.int32, shape1, 0) * A_QBLOCK
    k0 = jnp.clip(q0 - (A_LOCAL_WINDOW - A_QBLOCK) // 2, 0, s - A_LOCAL_WINDOW)
    d1 = jnp.abs(q0 + lax.broadcasted_iota(jnp.int32, shape1, 1) - k0 - lax.broadcasted_iota(jnp.int32, shape1, 2))
    sub = s // A_STRIDE
    d2 = jnp.abs(lax.broadcasted_iota(jnp.int32, (sub, sub), 0) - lax.broadcasted_iota(jnp.int32, (sub, sub), 1)) * A_STRIDE
    return d1.astype(F32), log_count(d1, local), d2.astype(F32), log_count(d2, strided)


def _softmax_pv(s, v):
    m = jnp.max(s, axis=-1, keepdims=True)
    p = jnp.exp(s - m)
    l = jnp.sum(p, axis=-1, keepdims=True)
    o = jnp.dot(p.astype(BF16), v, preferred_element_type=F32)
    return o / l, m + jnp.log(l)


def _attn_a_kernel(slopes_ref, qn_ref, kn_ref, vn_ref, d1_ref, l1_ref, d2_ref, l2_ref, o_ref, o2_scr, lse2_scr, wide_scr):
    ns = A_STRIDE
    for part, ref in enumerate((qn_ref, kn_ref, vn_ref)):
        wide_scr[part] = ref[...].astype(F32)
    slope = slopes_ref[pl.program_id(1)]
    s_len, hd = qn_ref.shape
    qb, win = A_QBLOCK, A_LOCAL_WINDOW
    nb = s_len // qb

    bias2 = l2_ref[...] - slope * d2_ref[...]
    for g in range(ns):
        rows = pl.ds(g, s_len // ns, stride=ns)
        q, k, v = (wide_scr[part, rows, :].astype(BF16) for part in range(3))
        scores = lax.dot_general(q, k, _NT, preferred_element_type=F32)
        out, lse = _softmax_pv(scores + bias2, v)
        o2_scr[rows, :] = out
        lse2_scr[rows, :] = jnp.broadcast_to(lse, (s_len // ns, hd))

    for grp in range(nb // A_GROUP):
        blocks = range(grp * A_GROUP, (grp + 1) * A_GROUP)
        starts = [min(max(b * qb - (win - qb) // 2, 0), s_len - win) for b in blocks]
        q = qn_ref[grp * A_GROUP * qb:(grp + 1) * A_GROUP * qb, :].reshape(A_GROUP, qb, hd)
        k = jnp.stack([kn_ref[k0:k0 + win, :] for k0 in starts])
        v = jnp.stack([vn_ref[k0:k0 + win, :] for k0 in starts])
        rows = slice(grp * A_GROUP, (grp + 1) * A_GROUP)
        scores = jnp.einsum("bqd,bkd->bqk", q, k, preferred_element_type=F32)
        scores = scores + (l1_ref[rows] - slope * d1_ref[rows])
        m = jnp.max(scores, axis=-1, keepdims=True)
        p = jnp.exp(scores - m)
        l = jnp.sum(p, axis=-1, keepdims=True)
        out1 = jnp.einsum("bqk,bkd->bqd", p.astype(BF16), v, preferred_element_type=F32) / l
        lse1 = m + jnp.log(l)
        tok = slice(grp * A_GROUP * qb, (grp + 1) * A_GROUP * qb)
        out2 = o2_scr[tok, :].reshape(A_GROUP, qb, hd)
        lse2 = lse2_scr[tok, 0:1].reshape(A_GROUP, qb, 1)
        top = jnp.maximum(lse1, lse2)
        w1, w2 = jnp.exp(lse1 - top), jnp.exp(lse2 - top)
        out = (w1 * out1 + w2 * out2) / (w1 + w2)
        o_ref[tok, :] = out.reshape(A_GROUP * qb, hd).astype(o_ref.dtype)


def _attention_a(zqkv, tables, n_seq, s, n_heads):
    hd = HEAD_DIM

    def natural(part):
        return pl.BlockSpec((s, hd), lambda b, h: (b, part * n_heads + h))

    def whole(t):
        return pl.BlockSpec(t.shape, lambda b, h: (0,) * t.ndim)

    return pl.pallas_call(
        _attn_a_kernel,
        grid=(n_seq, n_heads),
        in_specs=[pl.BlockSpec(memory_space=pltpu.SMEM), natural(0), natural(1), natural(2)] + [whole(t) for t in tables],
        out_specs=pl.BlockSpec((s, hd), lambda b, h: (b, h)),
        out_shape=jax.ShapeDtypeStruct((n_seq * s, n_heads * hd), BF16),
        scratch_shapes=[pltpu.VMEM((s, hd), F32), pltpu.VMEM((s, hd), F32), pltpu.VMEM((3, s, hd), F32)],
        compiler_params=_params("parallel", "parallel"),
        name="attn_a",
    )(_alibi_slopes(n_heads), zqkv, zqkv, zqkv, *tables)


def _attn_b_kernel(lam_init, slopes_ref, q_ref, k_ref, v_ref, dist_ref,
                   lq1_ref, lk1_ref, lq2_ref, lk2_ref, g_ref, o_ref):
    hd = HEAD_DIM
    bias = -slopes_ref[pl.program_id(2)] * dist_ref[...]
    lam = (jnp.exp(jnp.sum(lq1_ref[...] * lk1_ref[...], axis=-1, keepdims=True))
           - jnp.exp(jnp.sum(lq2_ref[...] * lk2_ref[...], axis=-1, keepdims=True)) + lam_init)

    def softmax_parts(q, k):
        s = lax.dot_general(q, k, _NT, preferred_element_type=F32) + bias
        p = jnp.exp(s - jnp.max(s, axis=-1, keepdims=True))
        return p, jnp.sum(p, axis=-1, keepdims=True)

    p1, l1 = softmax_parts(q_ref[:, :hd], k_ref[:, :hd])
    p2, l2 = softmax_parts(q_ref[:, hd:], k_ref[:, hd:])
    a = p1 * (1.0 / l1) - p2 * (lam / l2)
    o = jnp.dot(a.astype(BF16), v_ref[...], preferred_element_type=F32)
    ms = jnp.mean(o * o, axis=-1, keepdims=True)
    y = (o * lax.rsqrt(ms + NORM_EPS) * g_ref[...]) * (1.0 - lam_init)
    o_ref[...] = y.astype(o_ref.dtype)


def _attention_b(zqkv, dist, lam_vecs, g_subln, lam_init, n_seq, s, n_heads, col0, tq=512):
    tq = min(tq, s)
    nq = s // tq
    w = 2 * HEAD_DIM
    c0 = col0 // w
    vec = pl.BlockSpec((1, HEAD_DIM), lambda qi, b, h: (0, 0))
    return pl.pallas_call(
        functools.partial(_attn_b_kernel, lam_init),
        grid=(nq, n_seq, n_heads),
        in_specs=[
            pl.BlockSpec(memory_space=pltpu.SMEM),
            pl.BlockSpec((tq, w), lambda qi, b, h: (b * nq + qi, c0 + h)),
            pl.BlockSpec((s, w), lambda qi, b, h: (b, c0 + n_heads + h)),
            pl.BlockSpec((s, w), lambda qi, b, h: (b, c0 + 2 * n_heads + h)),
            pl.BlockSpec((tq, s), lambda qi, b, h: (qi, 0)),
            vec, vec, vec, vec,
            pl.BlockSpec((1, w), lambda qi, b, h: (0, 0)),
        ],
        out_specs=pl.BlockSpec((tq, w), lambda qi, b, h: (b * nq + qi, h)),
        out_shape=jax.ShapeDtypeStruct((n_seq * s, n_heads * w), BF16),
        compiler_params=_params("parallel", "parallel", "parallel"),
        name="attn_b",
    )(_alibi_slopes(n_heads), zqkv, zqkv, zqkv, dist,
      *[v.reshape(1, HEAD_DIM) for v in lam_vecs], g_subln.reshape(1, w))


def _merge_kernel(xn_ref, oa_ref, ob_ref, wga_ref, wgb_ref, wpa_ref, wpb_ref, o_ref):
    xn = xn_ref[...]
    ga = jnp.dot(xn, wga_ref[...], preferred_element_type=F32)
    gb = jnp.dot(xn, wgb_ref[...], preferred_element_type=F32)
    pa = jnp.dot(oa_ref[...], wpa_ref[...], preferred_element_type=F32)
    pb = jnp.dot(ob_ref[...], wpb_ref[...], preferred_element_type=F32)
    o_ref[...] = (jax.nn.sigmoid(ga) * pa + jax.nn.sigmoid(gb) * pb).astype(o_ref.dtype)


def _merge(xn, oa, ob, w_in, w_pa, w_pb, gate_col0, tm=512, tn=512):
    t, d = xn.shape
    tm, tn = min(tm, t), min(tn, d)
    ga0 = gate_col0 // tn
    gb0 = (gate_col0 + d) // tn
    return pl.pallas_call(
        _merge_kernel,
        grid=(t // tm, d // tn),
        in_specs=[
            pl.BlockSpec((tm, d), lambda i, j: (i, 0)),
            pl.BlockSpec((tm, oa.shape[1]), lambda i, j: (i, 0)),
            pl.BlockSpec((tm, ob.shape[1]), lambda i, j: (i, 0)),
            pl.BlockSpec((d, tn), lambda i, j: (0, ga0 + j)),
            pl.BlockSpec((d, tn), lambda i, j: (0, gb0 + j)),
            pl.BlockSpec((w_pa.shape[0], tn), lambda i, j: (0, j)),
            pl.BlockSpec((w_pb.shape[0], tn), lambda i, j: (0, j)),
        ],
        out_specs=pl.BlockSpec((tm, tn), lambda i, j: (i, j)),
        out_shape=jax.ShapeDtypeStruct((t, d), BF16),
        compiler_params=_params("parallel", "parallel"),
        name="merge",
    )(xn, oa, ob, w_in, w_in, w_pa, w_pb)


SUBLANES = 8


def _sort_network(n):
    def merge(lo, hi, r):
        step = 2 * r
        if step < hi - lo:
            yield from merge(lo, hi, step)
            yield from merge(lo + r, hi, step)
            yield from ((i, i + r) for i in range(lo + r, hi - r, step))
        else:
            yield (lo, lo + r)

    def sort(lo, hi):
        if hi - lo >= 1:
            mid = lo + (hi - lo) // 2
            yield from sort(lo, mid)
            yield from sort(mid + 1, hi)
            yield from merge(lo, hi, 1)

    return list(sort(0, n - 1))


def _top_values(x, n):
    rows, cols = x.shape
    depth = pl.next_power_of_2(rows // SUBLANES)
    tiles = [x[i * SUBLANES:(i + 1) * SUBLANES, :] for i in range(rows // SUBLANES)]
    tiles += [jnp.full((SUBLANES, cols), -jnp.inf, x.dtype)] * (depth - len(tiles))
    for i, j in _sort_network(depth):
        tiles[i], tiles[j] = jnp.maximum(tiles[i], tiles[j]), jnp.minimum(tiles[i], tiles[j])
    sub = lax.broadcasted_iota(jnp.int32, (SUBLANES, cols), 0)
    out = []
    for k in range(n):
        head = tiles[0]
        m = jnp.max(head, axis=0, keepdims=True)
        out.append(m)
        first = jnp.min(jnp.where(head == m, sub, SUBLANES), axis=0, keepdims=True)
        pop = sub == first
        for v in range(min(n - k - 1, depth)):
            below = tiles[v + 1] if v + 1 < depth else -jnp.inf
            tiles[v] = jnp.where(pop, below, tiles[v])
    return out


def _peer_score_kernel(q_ref, k1_ref, k2_ref, thr_ref, c_ref, s2_ref, e2_ref, v1_scr, v2_scr):
    n_heads, nk, c = k1_ref.shape
    tm = q_ref.shape[0]
    k = PK_TOPK
    pad = jnp.full((SUBLANES - 1, tm), -jnp.inf, F32)
    for h in range(n_heads):
        s1 = lax.dot_general(k1_ref[h], q_ref[:, 2 * h * c:(2 * h + 1) * c], _NT, preferred_element_type=F32)
        s2 = lax.dot_general(k2_ref[h], q_ref[:, (2 * h + 1) * c:(2 * h + 2) * c], _NT, preferred_element_type=F32)
        v1_scr[...] = jnp.concatenate(_top_values(s1, k + 1) + [pad], axis=0)
        v2_scr[...] = jnp.concatenate(_top_values(s2, k + 1) + [pad], axis=0)
        pieces = [v1_scr[0:1, :] + v2_scr[0:k, :]]
        for a in range(1, 8):
            pieces.append(v1_scr[a:a + 1, :] + v2_scr[0:8, :])
        pieces.append(v1_scr[8:k, :] + v2_scr[0:1, :])
        pieces.append(v1_scr[k:k + 8, :] + v2_scr[0:1, :])
        pieces.append(v1_scr[0:1, :] + v2_scr[k:k + 8, :])
        tops = _top_values(jnp.concatenate(pieces, axis=0), k + 1)
        tau = 0.5 * (tops[k - 1] + tops[k])
        z = jnp.exp(tops[0] - tops[0])
        for t in tops[1:k]:
            z = z + jnp.exp(t - tops[0])
        thr_ref[h] = tau - s1
        c_ref[h] = jnp.exp(s1 - v1_scr[0:1, :]) / z
        s2_ref[h] = s2
        e2_ref[h] = jnp.exp(s2 - v2_scr[0:1, :])


def _peer_scores(q, k1, k2, tm=512):
    t = q.shape[0]
    n_heads, nk, c = k1.shape
    tm = min(tm, t)
    out = jax.ShapeDtypeStruct((n_heads, nk, t), F32)
    ospec = pl.BlockSpec((n_heads, nk, tm), lambda i: (0, 0, i))
    kspec = pl.BlockSpec((n_heads, nk, c), lambda i: (0, 0, 0))
    return pl.pallas_call(
        _peer_score_kernel,
        grid=(t // tm,),
        in_specs=[pl.BlockSpec((tm, q.shape[1]), lambda i: (i, 0)), kspec, kspec],
        out_specs=[ospec] * 4,
        out_shape=[out] * 4,
        scratch_shapes=[pltpu.VMEM((PK_TOPK + 8, tm), F32), pltpu.VMEM((PK_TOPK + 8, tm), F32)],
        compiler_params=_params("parallel"),
        name="peer_scores",
    )(q, k1, k2)


def _peer_expert_kernel(xn_ref, u_ref, v_ref, thr_ref, c_ref, s2_ref, e2_ref, o_ref):
    j = pl.program_id(1)
    n_heads, nk, tm = s2_ref.shape
    te = u_ref.shape[0]
    @pl.when(j == 0)
    def _():
        o_ref[...] = jnp.zeros_like(o_ref)

    gates = []
    for r in range(te // nk):
        i1 = j * (te // nk) + r
        g_t = jnp.zeros((nk, tm), F32)
        for h in range(n_heads):
            thr = thr_ref[h, pl.ds(i1, 1), :]
            coef = c_ref[h, pl.ds(i1, 1), :]
            g_t = g_t + jnp.where(s2_ref[h] >= thr, e2_ref[h] * coef, 0.0)
        gates.append(g_t.T)
    hidden = lax.dot_general(xn_ref[...], u_ref[...], _NT, preferred_element_type=F32)
    a = (jax.nn.gelu(hidden) * jnp.concatenate(gates, axis=1)).astype(BF16)
    o_ref[...] += jnp.dot(a, v_ref[...], preferred_element_type=F32)


def _peer_experts(xn, u, v, thr, coef, s2, e2, tm=512, te=512):
    t, d = xn.shape
    n_exp = u.shape[0]
    n_heads, nk, _ = s2.shape
    tm, te = min(tm, t), min(te, n_exp)
    once = pl.Buffered(1)
    gspec = pl.BlockSpec((n_heads, nk, tm), lambda i, j: (0, 0, i), pipeline_mode=once)
    return pl.pallas_call(
        _peer_expert_kernel,
        grid=(t // tm, n_exp // te),
        in_specs=[
            pl.BlockSpec((tm, d), lambda i, j: (i, 0), pipeline_mode=once),
            pl.BlockSpec((te, d), lambda i, j: (j, 0)),
            pl.BlockSpec((te, d), lambda i, j: (j, 0)),
            gspec, gspec, gspec, gspec,
        ],
        out_specs=pl.BlockSpec((tm, d), lambda i, j: (i, 0)),
        out_shape=jax.ShapeDtypeStruct((t, d), F32),
        compiler_params=_params("parallel", "arbitrary"),
        name="peer_experts",
    )(xn, u, v, thr, coef, s2, e2)


def kernel(x_prompt, x_sample, w_in, w_proj_a, w_proj_b, w_out, g_mix_norm, lam_q1, lam_k1, lam_q2,
           lam_k2, g_subln, g_ffn_norm, w_query, sub_keys_1, sub_keys_2, expert_u, expert_v, g_final):
    s, d = x_prompt.shape[1], x_prompt.shape[2]
    assert x_sample.shape[1] == s and all(s % (w // 2) == 0 for w, _ in DILATED_PATTERNS)
    n_seq = x_prompt.shape[0] + x_sample.shape[0]
    depth = w_in.shape[0]
    a_width, b_width = w_proj_a.shape[1], w_proj_b.shape[1]
    a_heads, b_heads = a_width // HEAD_DIM, b_width // (2 * HEAD_DIM)
    qkv_width = 3 * a_width + 3 * b_width
    scale = HEAD_DIM ** -0.5

    x_chunks = [x_prompt.reshape(-1, d), x_sample.reshape(-1, d)]
    out_rows = [c.shape[0] for c in x_chunks]
    dist = _distance_table(s)
    tables_a = _attn_a_tables(s)
    one_a, one_b = jnp.ones((2 * a_width,), F32), jnp.ones((2 * b_width,), F32)
    colscale = jnp.concatenate([jnp.full((a_width,), scale, F32), one_a, jnp.full((b_width,), scale, F32), one_b])

    for l in range(depth):
        lam_init = 0.8 - 0.6 * math.exp(-0.3 * l)
        w_in_l = w_in[l].astype(BF16)
        xn = _rmsnorm(x_chunks, g_mix_norm[l], BF16)
        zqkv = _matmul(xn, w_in_l, BF16, n_cols=qkv_width, colscale=colscale, name="in_proj")
        o_a = _attention_a(zqkv, tables_a, n_seq, s, a_heads)
        o_b = _attention_b(zqkv, dist, (lam_q1[l], lam_k1[l], lam_q2[l], lam_k2[l]), g_subln[l], lam_init,
                           n_seq, s, b_heads, 3 * a_width)
        merged = _merge(xn, o_a, o_b, w_in_l, w_proj_a[l].astype(BF16), w_proj_b[l].astype(BF16), qkv_width)
        x1 = _matmul(merged, w_out[l].astype(BF16), F32, residual=x_chunks, name="out_proj")
        xn2 = _rmsnorm([x1], g_ffn_norm[l], BF16)
        q = _matmul(xn2, w_query[l].astype(BF16), BF16, name="peer_query")
        thr, coef, s2, e2 = _peer_scores(q, sub_keys_1[l].astype(BF16), sub_keys_2[l].astype(BF16))
        peer = _peer_experts(xn2, expert_u[l].astype(BF16), expert_v[l].astype(BF16), thr, coef, s2, e2)
        if l + 1 < depth:
            x_chunks = [x1 + peer]
    y_prompt, y_sample = _add_rmsnorm_split(x1, peer, g_final, out_rows, F32)
    return y_prompt.reshape(x_prompt.shape), y_sample.reshape(x_sample.shape)
```

```python
import functools
import math

import numpy as np
import jax
import jax.numpy as jnp
from jax import lax
from jax.experimental import pallas as pl
from jax.experimental.pallas import tpu as pltpu

HEAD_DIM = 128
DILATED_PATTERNS = ((128, 1), (512, 4), (2048, 16))
PK_TOPK = 16
NORM_EPS = 1e-6
NEG_INF = -1e30
MIB = 1024 * 1024
V7X_VMEM_BYTES = 64 * MIB
VMEM_RESERVE_BYTES = 8 * MIB
VMEM_LIMIT_BYTES = V7X_VMEM_BYTES - VMEM_RESERVE_BYTES

ROW_TILE = 512
NORM_ROW_TILE = 256
PROJ_COL_TILE = 1024
MERGE_COL_TILE = 512
EXPERT_TILE = 512
ATTN_B_QTILE = 1024

F32 = jnp.float32
BF16 = jnp.bfloat16
_NT = (((1,), (1,)), ((), ()))


def _params(*sem):
    return pltpu.CompilerParams(dimension_semantics=sem, vmem_limit_bytes=VMEM_LIMIT_BYTES)


def _chunk_specs(chunks, tm, tn, row_col):
    specs, bounds, lo = [], [], 0
    for rows in chunks:
        assert rows % tm == 0, (rows, tm)
        n = rows // tm

        def index_map(*g, lo=lo, n=n):
            i, j = row_col(*g)
            return (jnp.clip(i - lo, 0, n - 1), j)

        specs.append(pl.BlockSpec((tm, tn), index_map))
        bounds.append((lo, lo + n))
        lo += n
    return specs, bounds


def _for_active_chunk(i, refs, bounds, body):
    if len(refs) == 1:
        body(refs[0])
        return
    for ref, (lo, hi) in zip(refs, bounds):
        pl.when((i >= lo) & (i < hi))(functools.partial(body, ref))


def _rms(x, g):
    ms = jnp.mean(x * x, axis=-1, keepdims=True)
    return x * lax.rsqrt(ms + NORM_EPS) * g


def _rmsnorm_kernel(bounds, *refs):
    *x_refs, g_ref, o_ref = refs

    def body(x_ref):
        o_ref[...] = _rms(x_ref[...], g_ref[...]).astype(o_ref.dtype)

    _for_active_chunk(pl.program_id(0), x_refs, bounds, body)


def _rmsnorm(chunks, g, out_dtype, tm=NORM_ROW_TILE):
    d = chunks[0].shape[1]
    t = sum(c.shape[0] for c in chunks)
    specs, bounds = _chunk_specs([c.shape[0] for c in chunks], tm, d, lambda i: (i, 0))
    return pl.pallas_call(
        functools.partial(_rmsnorm_kernel, bounds),
        grid=(t // tm,),
        in_specs=specs + [pl.BlockSpec((1, d), lambda i: (0, 0))],
        out_specs=pl.BlockSpec((tm, d), lambda i: (i, 0)),
        out_shape=jax.ShapeDtypeStruct((t, d), out_dtype),
        compiler_params=_params("parallel"),
        name="rmsnorm",
    )(*chunks, g.reshape(1, d))


def _add_rmsnorm_kernel(bounds, a_ref, b_ref, g_ref, *o_refs):
    y = _rms(a_ref[...] + b_ref[...], g_ref[...])

    def body(o_ref):
        o_ref[...] = y.astype(o_ref.dtype)

    _for_active_chunk(pl.program_id(0), o_refs, bounds, body)


def _add_rmsnorm_split(a, b, g, out_rows, out_dtype, tm=NORM_ROW_TILE):
    t, d = a.shape
    row = pl.BlockSpec((tm, d), lambda i: (i, 0))
    specs, bounds = _chunk_specs(out_rows, tm, d, lambda i: (i, 0))
    return pl.pallas_call(
        functools.partial(_add_rmsnorm_kernel, bounds),
        grid=(t // tm,),
        in_specs=[row, row, pl.BlockSpec((1, d), lambda i: (0, 0))],
        out_specs=specs,
        out_shape=[jax.ShapeDtypeStruct((r, d), out_dtype) for r in out_rows],
        compiler_params=_params("arbitrary"),
        name="add_rmsnorm",
    )(a, b, g.reshape(1, d))


def _mm_kernel(a_ref, b_ref, o_ref):
    o_ref[...] = jnp.dot(a_ref[...], b_ref[...], preferred_element_type=F32).astype(o_ref.dtype)


def _mm_colscale_kernel(a_ref, b_ref, cs_ref, o_ref):
    acc = jnp.dot(a_ref[...], b_ref[...], preferred_element_type=F32)
    o_ref[...] = (acc * cs_ref[...]).astype(o_ref.dtype)


def _mm_residual_kernel(bounds, a_ref, b_ref, *refs):
    *r_refs, o_ref = refs
    acc = jnp.dot(a_ref[...], b_ref[...], preferred_element_type=F32)

    def body(r_ref):
        o_ref[...] = (r_ref[...] + acc).astype(o_ref.dtype)

    _for_active_chunk(pl.program_id(1), r_refs, bounds, body)


def _matmul(a, b, out_dtype, *, n_cols=None, colscale=None, residual=None, tm=ROW_TILE, tn=PROJ_COL_TILE, name="matmul"):
    m, k = a.shape
    n = b.shape[1] if n_cols is None else n_cols
    tm, tn = min(tm, m), min(tn, n)
    assert m % tm == 0 and n % tn == 0, (m, n, tm, tn)
    a_spec = pl.BlockSpec((tm, k), lambda j, i: (i, 0))
    b_spec = pl.BlockSpec((k, tn), lambda j, i: (0, j))
    o_spec = pl.BlockSpec((tm, tn), lambda j, i: (i, j))
    if colscale is not None:
        kern, ins = _mm_colscale_kernel, (a, b, colscale.reshape(1, n))
        specs = [a_spec, b_spec, pl.BlockSpec((1, tn), lambda j, i: (0, j))]
    elif residual is not None:
        r_specs, bounds = _chunk_specs([r.shape[0] for r in residual], tm, tn, lambda j, i: (i, j))
        kern, ins, specs = functools.partial(_mm_residual_kernel, bounds), (a, b, *residual), [a_spec, b_spec] + r_specs
    else:
        kern, ins, specs = _mm_kernel, (a, b), [a_spec, b_spec]
    return pl.pallas_call(
        kern,
        grid=(n // tn, m // tm),
        in_specs=specs,
        out_specs=o_spec,
        out_shape=jax.ShapeDtypeStruct((m, n), out_dtype),
        compiler_params=_params("parallel", "parallel"),
        name=name,
    )(*ins)


def _distance_table(s):
    i = lax.broadcasted_iota(jnp.int32, (s, s), 0)
    j = lax.broadcasted_iota(jnp.int32, (s, s), 1)
    return jnp.abs(i - j).astype(F32)


def _alibi_slopes(n):
    return jnp.asarray(2.0 ** (-8.0 * np.arange(1, n + 1) / n), dtype=F32)


A_STRIDE = 4
A_QBLOCK = 128
A_LOCAL_WINDOW = 2 * A_QBLOCK
A_GROUP = 4


def _attn_a_tables(s):
    local = [(w, d) for w, d in DILATED_PATTERNS if d == 1]
    strided = [(w, d) for w, d in DILATED_PATTERNS if d != 1]
    assert all(d % A_STRIDE == 0 for _, d in strided) and all(w // 2 <= A_QBLOCK // 2 for w, _ in local)

    def log_count(dist, patterns):
        count = jnp.zeros(dist.shape, F32)
        for window, dilation in patterns:
            count = count + ((dist % dilation == 0) & (dist <= window // 2)).astype(F32)
        return jnp.where(count > 0, jnp.log(jnp.maximum(count, 1.0)), NEG_INF)

    nb = s // A_QBLOCK
    shape1 = (nb, A_QBLOCK, A_LOCAL_WINDOW)
    q0 = lax.broadcasted_iota(jnp.int32, shape1, 0) * A_QBLOCK
    k0 = jnp.clip(q0 - (A_LOCAL_WINDOW - A_QBLOCK) // 2, 0, s - A_LOCAL_WINDOW)
    d1 = jnp.abs(q0 + lax.broadcasted_iota(jnp.int32, shape1, 1) - k0 - lax.broadcasted_iota(jnp.int32, shape1, 2))
    sub = s // A_STRIDE
    d2 = jnp.abs(lax.broadcasted_iota(jnp.int32, (sub, sub), 0) - lax.broadcasted_iota(jnp.int32, (sub, sub), 1)) * A_STRIDE
    return d1.astype(F32), log_count(d1, local), d2.astype(F32), log_count(d2, strided)


def _softmax_pv(s, v):
    m = jnp.max(s, axis=-1, keepdims=True)
    p = jnp.exp(s - m)
    l = jnp.sum(p, axis=-1, keepdims=True)
    o = jnp.dot(p.astype(BF16), v, preferred_element_type=F32)
    return o / l, m + jnp.log(l)


def _attn_a_kernel(slopes_ref, qn_ref, kn_ref, vn_ref, d1_ref, l1_ref, d2_ref, l2_ref, o_ref, o2_scr, lse2_scr, wide_scr):
    ns = A_STRIDE
    for part, ref in enumerate((qn_ref, kn_ref, vn_ref)):
        wide_scr[part] = ref[...].astype(F32)
    slope = slopes_ref[pl.program_id(1)]
    s_len, hd = qn_ref.shape
    qb, win = A_QBLOCK, A_LOCAL_WINDOW
    nb = s_len // qb

    bias2 = l2_ref[...] - slope * d2_ref[...]
    for g in range(ns):
        rows = pl.ds(g, s_len // ns, stride=ns)
        q, k, v = (wide_scr[part, rows, :].astype(BF16) for part in range(3))
        scores = lax.dot_general(q, k, _NT, preferred_element_type=F32)
        out, lse = _softmax_pv(scores + bias2, v)
        o2_scr[rows, :] = out
        lse2_scr[rows, :] = jnp.broadcast_to(lse, (s_len // ns, hd))

    for grp in range(nb // A_GROUP):
        blocks = range(grp * A_GROUP, (grp + 1) * A_GROUP)
        starts = [min(max(b * qb - (win - qb) // 2, 0), s_len - win) for b in blocks]
        q = qn_ref[grp * A_GROUP * qb:(grp + 1) * A_GROUP * qb, :].reshape(A_GROUP, qb, hd)
        k = jnp.stack([kn_ref[k0:k0 + win, :] for k0 in starts])
        v = jnp.stack([vn_ref[k0:k0 + win, :] for k0 in starts])
        rows = slice(grp * A_GROUP, (grp + 1) * A_GROUP)
        scores = jnp.einsum("bqd,bkd->bqk", q, k, preferred_element_type=F32)
        scores = scores + (l1_ref[rows] - slope * d1_ref[rows])
        m = jnp.max(scores, axis=-1, keepdims=True)
        p = jnp.exp(scores - m)
        l = jnp.sum(p, axis=-1, keepdims=True)
        out1 = jnp.einsum("bqk,bkd->bqd", p.astype(BF16), v, preferred_element_type=F32) / l
        lse1 = m + jnp.log(l)
        tok = slice(grp * A_GROUP * qb, (grp + 1) * A_GROUP * qb)
        out2 = o2_scr[tok, :].reshape(A_GROUP, qb, hd)
        lse2 = lse2_scr[tok, 0:1].reshape(A_GROUP, qb, 1)
        top = jnp.maximum(lse1, lse2)
        w1, w2 = jnp.exp(lse1 - top), jnp.exp(lse2 - top)
        out = (w1 * out1 + w2 * out2) / (w1 + w2)
        o_ref[tok, :] = out.reshape(A_GROUP * qb, hd).astype(o_ref.dtype)


def _attention_a(zqkv, tables, n_seq, s, n_heads):
    hd = HEAD_DIM

    def natural(part):
        return pl.BlockSpec((s, hd), lambda b, h: (b, part * n_heads + h))

    def whole(t):
        return pl.BlockSpec(t.shape, lambda b, h: (0,) * t.ndim)

    return pl.pallas_call(
        _attn_a_kernel,
        grid=(n_seq, n_heads),
        in_specs=[pl.BlockSpec(memory_space=pltpu.SMEM), natural(0), natural(1), natural(2)] + [whole(t) for t in tables],
        out_specs=pl.BlockSpec((s, hd), lambda b, h: (b, h)),
        out_shape=jax.ShapeDtypeStruct((n_seq * s, n_heads * hd), BF16),
        scratch_shapes=[pltpu.VMEM((s, hd), F32), pltpu.VMEM((s, hd), F32), pltpu.VMEM((3, s, hd), F32)],
        compiler_params=_params("parallel", "parallel"),
        name="attn_a",
    )(_alibi_slopes(n_heads), zqkv, zqkv, zqkv, *tables)


def _attn_b_kernel(lam_init, slopes_ref, q_ref, k_ref, v_ref, dist_ref,
                   lq1_ref, lk1_ref, lq2_ref, lk2_ref, g_ref, o_ref):
    hd = HEAD_DIM
    bias = -slopes_ref[pl.program_id(2)] * dist_ref[...]
    lam = (jnp.exp(jnp.sum(lq1_ref[...] * lk1_ref[...], axis=-1, keepdims=True))
           - jnp.exp(jnp.sum(lq2_ref[...] * lk2_ref[...], axis=-1, keepdims=True)) + lam_init)

    def softmax_parts(q, k):
        s = lax.dot_general(q, k, _NT, preferred_element_type=F32) + bias
        p = jnp.exp(s - jnp.max(s, axis=-1, keepdims=True))
        return p, jnp.sum(p, axis=-1, keepdims=True)

    p1, l1 = softmax_parts(q_ref[:, :hd], k_ref[:, :hd])
    p2, l2 = softmax_parts(q_ref[:, hd:], k_ref[:, hd:])
    a = p1 * (1.0 / l1) - p2 * (lam / l2)
    o = jnp.dot(a.astype(BF16), v_ref[...], preferred_element_type=F32)
    ms = jnp.mean(o * o, axis=-1, keepdims=True)
    y = (o * lax.rsqrt(ms + NORM_EPS) * g_ref[...]) * (1.0 - lam_init)
    o_ref[...] = y.astype(o_ref.dtype)


def _attention_b(zqkv, dist, lam_vecs, g_subln, lam_init, n_seq, s, n_heads, col0, tq=ATTN_B_QTILE):
    tq = min(tq, s)
    nq = s // tq
    w = 2 * HEAD_DIM
    c0 = col0 // w
    vec = pl.BlockSpec((1, HEAD_DIM), lambda qi, b, h: (0, 0))
    return pl.pallas_call(
        functools.partial(_attn_b_kernel, lam_init),
        grid=(nq, n_seq, n_heads),
        in_specs=[
            pl.BlockSpec(memory_space=pltpu.SMEM),
            pl.BlockSpec((tq, w), lambda qi, b, h: (b * nq + qi, c0 + h)),
            pl.BlockSpec((s, w), lambda qi, b, h: (b, c0 + n_heads + h)),
            pl.BlockSpec((s, w), lambda qi, b, h: (b, c0 + 2 * n_heads + h)),
            pl.BlockSpec((tq, s), lambda qi, b, h: (qi, 0)),
            vec, vec, vec, vec,
            pl.BlockSpec((1, w), lambda qi, b, h: (0, 0)),
        ],
        out_specs=pl.BlockSpec((tq, w), lambda qi, b, h: (b * nq + qi, h)),
        out_shape=jax.ShapeDtypeStruct((n_seq * s, n_heads * w), BF16),
        compiler_params=_params("parallel", "parallel", "parallel"),
        name="attn_b",
    )(_alibi_slopes(n_heads), zqkv, zqkv, zqkv, dist,
      *[v.reshape(1, HEAD_DIM) for v in lam_vecs], g_subln.reshape(1, w))


def _merge_kernel(xn_ref, oa_ref, ob_ref, wga_ref, wgb_ref, wpa_ref, wpb_ref, o_ref):
    xn = xn_ref[...]
    ga = jnp.dot(xn, wga_ref[...], preferred_element_type=F32)
    gb = jnp.dot(xn, wgb_ref[...], preferred_element_type=F32)
    pa = jnp.dot(oa_ref[...], wpa_ref[...], preferred_element_type=F32)
    pb = jnp.dot(ob_ref[...], wpb_ref[...], preferred_element_type=F32)
    o_ref[...] = (jax.nn.sigmoid(ga) * pa + jax.nn.sigmoid(gb) * pb).astype(o_ref.dtype)


def _merge(xn, oa, ob, w_in, w_pa, w_pb, gate_col0, tm=ROW_TILE, tn=MERGE_COL_TILE):
    t, d = xn.shape
    tm, tn = min(tm, t), min(tn, d)
    ga0 = gate_col0 // tn
    gb0 = (gate_col0 + d) // tn
    return pl.pallas_call(
        _merge_kernel,
        grid=(t // tm, d // tn),
        in_specs=[
            pl.BlockSpec((tm, d), lambda i, j: (i, 0)),
            pl.BlockSpec((tm, oa.shape[1]), lambda i, j: (i, 0)),
            pl.BlockSpec((tm, ob.shape[1]), lambda i, j: (i, 0)),
            pl.BlockSpec((d, tn), lambda i, j: (0, ga0 + j)),
            pl.BlockSpec((d, tn), lambda i, j: (0, gb0 + j)),
            pl.BlockSpec((w_pa.shape[0], tn), lambda i, j: (0, j)),
            pl.BlockSpec((w_pb.shape[0], tn), lambda i, j: (0, j)),
        ],
        out_specs=pl.BlockSpec((tm, tn), lambda i, j: (i, j)),
        out_shape=jax.ShapeDtypeStruct((t, d), BF16),
        compiler_params=_params("parallel", "parallel"),
        name="merge",
    )(xn, oa, ob, w_in, w_in, w_pa, w_pb)


SUBLANES = 8


def _sort_network(n):
    def merge(lo, hi, r):
        step = 2 * r
        if step < hi - lo:
            yield from merge(lo, hi, step)
            yield from merge(lo + r, hi, step)
            yield from ((i, i + r) for i in range(lo + r, hi - r, step))
        else:
            yield (lo, lo + r)

    def sort(lo, hi):
        if hi - lo >= 1:
            mid = lo + (hi - lo) // 2
            yield from sort(lo, mid)
            yield from sort(mid + 1, hi)
            yield from merge(lo, hi, 1)

    return list(sort(0, n - 1))


def _top_values(x, n):
    rows, cols = x.shape
    depth = pl.next_power_of_2(rows // SUBLANES)
    tiles = [x[i * SUBLANES:(i + 1) * SUBLANES, :] for i in range(rows // SUBLANES)]
    tiles += [jnp.full((SUBLANES, cols), -jnp.inf, x.dtype)] * (depth - len(tiles))
    for i, j in _sort_network(depth):
        tiles[i], tiles[j] = jnp.maximum(tiles[i], tiles[j]), jnp.minimum(tiles[i], tiles[j])
    sub = lax.broadcasted_iota(jnp.int32, (SUBLANES, cols), 0)
    out = []
    for k in range(n):
        head = tiles[0]
        m = jnp.max(head, axis=0, keepdims=True)
        out.append(m)
        first = jnp.min(jnp.where(head == m, sub, SUBLANES), axis=0, keepdims=True)
        pop = sub == first
        for v in range(min(n - k - 1, depth)):
            below = tiles[v + 1] if v + 1 < depth else -jnp.inf
            tiles[v] = jnp.where(pop, below, tiles[v])
    return out


def _peer_score_kernel(q_ref, k1_ref, k2_ref, thr_ref, c_ref, s2_ref, e2_ref, v1_scr, v2_scr):
    n_heads, nk, c = k1_ref.shape
    tm = q_ref.shape[0]
    k = PK_TOPK
    pad = jnp.full((SUBLANES - 1, tm), -jnp.inf, F32)
    for h in range(n_heads):
        s1 = lax.dot_general(k1_ref[h], q_ref[:, 2 * h * c:(2 * h + 1) * c], _NT, preferred_element_type=F32)
        s2 = lax.dot_general(k2_ref[h], q_ref[:, (2 * h + 1) * c:(2 * h + 2) * c], _NT, preferred_element_type=F32)
        v1_scr[...] = jnp.concatenate(_top_values(s1, k + 1) + [pad], axis=0)
        v2_scr[...] = jnp.concatenate(_top_values(s2, k + 1) + [pad], axis=0)
        pieces = [v1_scr[0:1, :] + v2_scr[0:k, :]]
        for a in range(1, 8):
            pieces.append(v1_scr[a:a + 1, :] + v2_scr[0:8, :])
        pieces.append(v1_scr[8:k, :] + v2_scr[0:1, :])
        pieces.append(v1_scr[k:k + 8, :] + v2_scr[0:1, :])
        pieces.append(v1_scr[0:1, :] + v2_scr[k:k + 8, :])
        tops = _top_values(jnp.concatenate(pieces, axis=0), k + 1)
        tau = 0.5 * (tops[k - 1] + tops[k])
        z = jnp.exp(tops[0] - tops[0])
        for t in tops[1:k]:
            z = z + jnp.exp(t - tops[0])
        thr_ref[h] = tau - s1
        c_ref[h] = jnp.exp(s1 - v1_scr[0:1, :]) / z
        s2_ref[h] = s2
        e2_ref[h] = jnp.exp(s2 - v2_scr[0:1, :])


def _peer_scores(q, k1, k2, tm=ROW_TILE):
    t = q.shape[0]
    n_heads, nk, c = k1.shape
    tm = min(tm, t)
    out = jax.ShapeDtypeStruct((n_heads, nk, t), F32)
    ospec = pl.BlockSpec((n_heads, nk, tm), lambda i: (0, 0, i))
    kspec = pl.BlockSpec((n_heads, nk, c), lambda i: (0, 0, 0))
    return pl.pallas_call(
        _peer_score_kernel,
        grid=(t // tm,),
        in_specs=[pl.BlockSpec((tm, q.shape[1]), lambda i: (i, 0)), kspec, kspec],
        out_specs=[ospec] * 4,
        out_shape=[out] * 4,
        scratch_shapes=[pltpu.VMEM((PK_TOPK + 8, tm), F32), pltpu.VMEM((PK_TOPK + 8, tm), F32)],
        compiler_params=_params("parallel"),
        name="peer_scores",
    )(q, k1, k2)


def _peer_expert_kernel(xn_ref, u_ref, v_ref, thr_ref, c_ref, s2_ref, e2_ref, o_ref):
    j = pl.program_id(1)
    n_heads, nk, tm = s2_ref.shape
    te = u_ref.shape[0]
    @pl.when(j == 0)
    def _():
        o_ref[...] = jnp.zeros_like(o_ref)

    gates = []
    for r in range(te // nk):
        i1 = j * (te // nk) + r
        g_t = jnp.zeros((nk, tm), F32)
        for h in range(n_heads):
            thr = thr_ref[h, pl.ds(i1, 1), :]
            coef = c_ref[h, pl.ds(i1, 1), :]
            g_t = g_t + jnp.where(s2_ref[h] >= thr, e2_ref[h] * coef, 0.0)
        gates.append(g_t.T)
    hidden = lax.dot_general(xn_ref[...], u_ref[...], _NT, preferred_element_type=F32)
    a = (jax.nn.gelu(hidden) * jnp.concatenate(gates, axis=1)).astype(BF16)
    o_ref[...] += jnp.dot(a, v_ref[...], preferred_element_type=F32)


def _peer_experts(xn, u, v, thr, coef, s2, e2, tm=ROW_TILE, te=EXPERT_TILE):
    t, d = xn.shape
    n_exp = u.shape[0]
    n_heads, nk, _ = s2.shape
    tm, te = min(tm, t), min(te, n_exp)
    once = pl.Buffered(1)
    gspec = pl.BlockSpec((n_heads, nk, tm), lambda i, j: (0, 0, i), pipeline_mode=once)
    return pl.pallas_call(
        _peer_expert_kernel,
        grid=(t // tm, n_exp // te),
        in_specs=[
            pl.BlockSpec((tm, d), lambda i, j: (i, 0), pipeline_mode=once),
            pl.BlockSpec((te, d), lambda i, j: (j, 0)),
            pl.BlockSpec((te, d), lambda i, j: (j, 0)),
            gspec, gspec, gspec, gspec,
        ],
        out_specs=pl.BlockSpec((tm, d), lambda i, j: (i, 0)),
        out_shape=jax.ShapeDtypeStruct((t, d), F32),
        compiler_params=_params("parallel", "arbitrary"),
        name="peer_experts",
    )(xn, u, v, thr, coef, s2, e2)


def kernel(x_prompt, x_sample, w_in, w_proj_a, w_proj_b, w_out, g_mix_norm, lam_q1, lam_k1, lam_q2,
           lam_k2, g_subln, g_ffn_norm, w_query, sub_keys_1, sub_keys_2, expert_u, expert_v, g_final):
    s, d = x_prompt.shape[1], x_prompt.shape[2]
    assert x_sample.shape[1] == s and all(s % (w // 2) == 0 for w, _ in DILATED_PATTERNS)
    n_seq = x_prompt.shape[0] + x_sample.shape[0]
    depth = w_in.shape[0]
    a_width, b_width = w_proj_a.shape[1], w_proj_b.shape[1]
    a_heads, b_heads = a_width // HEAD_DIM, b_width // (2 * HEAD_DIM)
    qkv_width = 3 * a_width + 3 * b_width
    scale = HEAD_DIM ** -0.5

    x_chunks = [x_prompt.reshape(-1, d), x_sample.reshape(-1, d)]
    out_rows = [c.shape[0] for c in x_chunks]
    dist = _distance_table(s)
    tables_a = _attn_a_tables(s)
    one_a, one_b = jnp.ones((2 * a_width,), F32), jnp.ones((2 * b_width,), F32)
    colscale = jnp.concatenate([jnp.full((a_width,), scale, F32), one_a, jnp.full((b_width,), scale, F32), one_b])

    for l in range(depth):
        lam_init = 0.8 - 0.6 * math.exp(-0.3 * l)
        w_in_l = w_in[l].astype(BF16)
        xn = _rmsnorm(x_chunks, g_mix_norm[l], BF16)
        zqkv = _matmul(xn, w_in_l, BF16, n_cols=qkv_width, colscale=colscale, name="in_proj")
        o_a = _attention_a(zqkv, tables_a, n_seq, s, a_heads)
        o_b = _attention_b(zqkv, dist, (lam_q1[l], lam_k1[l], lam_q2[l], lam_k2[l]), g_subln[l], lam_init,
                           n_seq, s, b_heads, 3 * a_width)
        merged = _merge(xn, o_a, o_b, w_in_l, w_proj_a[l].astype(BF16), w_proj_b[l].astype(BF16), qkv_width)
        x1 = _matmul(merged, w_out[l].astype(BF16), F32, residual=x_chunks, name="out_proj")
        xn2 = _rmsnorm([x1], g_ffn_norm[l], BF16)
        q = _matmul(xn2, w_query[l].astype(BF16), BF16, name="peer_query")
        thr, coef, s2, e2 = _peer_scores(q, sub_keys_1[l].astype(BF16), sub_keys_2[l].astype(BF16))
        peer = _peer_experts(xn2, expert_u[l].astype(BF16), expert_v[l].astype(BF16), thr, coef, s2, e2)
        if l + 1 < depth:
            x_chunks = [x1 + peer]
    y_prompt, y_sample = _add_rmsnorm_split(x1, peer, g_final, out_rows, F32)
    return y_prompt.reshape(x_prompt.shape), y_sample.reshape(x_sample.shape)
```
